```python
import math
import jax, jax.numpy as jnp
from jax import lax
import numpy as np

D_MODEL = 2048
BATCH = 2
SEQ = 16384
DEPTH = 1
DEC_BATCH = 32
DEC_SEQ = 16
PAST_LEN = 1024

CHUNK = 64
QBLK = 128
DIFF_WIDTH = D_MODEL // 2
DIFF_H = 8
DIFF_DH = DIFF_WIDTH // DIFF_H // 2
CONV_W = D_MODEL // 4
CONV_K = 3
MEM_WIDTH = D_MODEL // 4
MEM_H = 4
MEM_DH = MEM_WIDTH // MEM_H
N_MEM = 256
N_BUCKETS = 32
MAX_DISTANCE = 128
MIX_WIDTH = DIFF_WIDTH + CONV_W + MEM_WIDTH
ALPHA = (2 * DEPTH) ** 0.25
BETA = (8 * DEPTH) ** -0.25
EPS = 1e-5

PROJ_WIDTHS = (2 * DIFF_WIDTH // 2 * 1, DIFF_WIDTH, DIFF_WIDTH, CONV_W, CONV_W, CONV_W, MEM_WIDTH,
               DIFF_WIDTH, CONV_W, MEM_WIDTH)
PROJ_TOTAL = sum(PROJ_WIDTHS)
PROJ_SPLITS = tuple(int(s) for s in np.cumsum(PROJ_WIDTHS)[:-1])

kernel_name = "hybrid_diffattn_shortconv_mem_stream_step"


def layer_norm(x, g, b):
    xf = x.astype(jnp.float32)
    mu = jnp.mean(xf, axis=-1, keepdims=True)
    var = jnp.mean(jnp.square(xf - mu), axis=-1, keepdims=True)
    return ((xf - mu) * lax.rsqrt(var + EPS) * g.astype(jnp.float32) + b.astype(jnp.float32)).astype(x.dtype)


def rms_norm(x, g):
    xf = x.astype(jnp.float32)
    y = xf * lax.rsqrt(jnp.mean(jnp.square(xf), axis=-1, keepdims=True) + EPS)
    return (y * g.astype(jnp.float32)).astype(x.dtype)


def t5_bucket(rel):
    half = N_BUCKETS // 2
    max_exact = half // 2
    ret = jnp.where(rel > 0, half, 0)
    n = jnp.abs(rel)
    nf = jnp.maximum(n, 1).astype(jnp.float32)
    large = max_exact + (jnp.log(nf / max_exact) / math.log(MAX_DISTANCE / max_exact)
                         * (half - max_exact)).astype(jnp.int32)
    large = jnp.minimum(large, half - 1)
    return ret + jnp.where(n < max_exact, n, large)


def rel_bias(qpos, kpos, table):
    bucket = t5_bucket(kpos[None, :] - qpos[:, None])
    return jnp.transpose(jnp.take(table, bucket, axis=0), (2, 0, 1))


def chunk_mask(qpos, kpos):
    return (kpos[None, :] // CHUNK) <= (qpos[:, None] // CHUNK)


def diff_attention(q, k, v, bias, mask, lam):
    s = jnp.einsum('bqhcd,bkhcd->bchqk', q, k).astype(jnp.float32) * (DIFF_DH ** -0.5)
    s = s + bias.astype(jnp.float32)
    s = jnp.where(mask, s, -jnp.inf)
    p = jax.nn.softmax(s, axis=-1)
    a = p[:, 0] - lam * p[:, 1]
    return jnp.einsum('bhqk,bkhe->bqhe', a.astype(v.dtype), v)


def prompt_diff_attention(q, k, v, lam, table):
    b, s = q.shape[0], q.shape[1]
    nb = s // QBLK
    qb = q.reshape(b, nb, QBLK, DIFF_H, 2, DIFF_DH).swapaxes(0, 1)
    kpos = jnp.arange(s)

    def block(args):
        qi, i = args
        qpos = i * QBLK + jnp.arange(QBLK)
        return diff_attention(qi, k, v, rel_bias(qpos, kpos, table), chunk_mask(qpos, kpos), lam)

    o = lax.map(block, (qb, jnp.arange(nb)))
    return o.swapaxes(0, 1).reshape(b, s, DIFF_H, 2 * DIFF_DH)


def mem_attention(q, mk, mv):
    s = jnp.einsum('bqhd,bkhd->bhqk', q, mk).astype(jnp.float32) * (MEM_DH ** -0.5)
    p = jax.nn.softmax(s, axis=-1)
    o = jnp.einsum('bhqk,bkhd->bqhd', p.astype(mv.dtype), mv)
    return o.reshape(q.shape[0], q.shape[1], MEM_WIDTH)


def causal_conv(u_padded, w, s):
    y = w[0] * u_padded[:, 0:s]
    for j in range(1, CONV_K):
        y = y + w[j] * u_padded[:, j:j + s]
    return y


def in_projection(x, w_in):
    p = jnp.einsum('bsd,de->bse', x, w_in)
    q, k, v, h, bg, cg, mq, g_d, g_c, g_m = jnp.split(p, PROJ_SPLITS, axis=-1)
    b, s = x.shape[0], x.shape[1]
    q = q.reshape(b, s, DIFF_H, 2, DIFF_DH)
    k = k.reshape(b, s, DIFF_H, 2, DIFF_DH)
    v = v.reshape(b, s, DIFF_H, 2 * DIFF_DH)
    mq = mq.reshape(b, s, MEM_H, MEM_DH)
    return q, k, v, h, bg, cg, mq, g_d, g_c, g_m


def diff_lambda(lq1, lk1, lq2, lk2, lam_init):
    f = jnp.float32
    return (jnp.exp(jnp.sum(lq1.astype(f) * lk1.astype(f))) -
            jnp.exp(jnp.sum(lq2.astype(f) * lk2.astype(f))) + lam_init)


def merge_and_norm(x, o_diff, conv_y, o_mem, g_d, g_c, g_m, subln_g, lam_init, w_out, ln_g, ln_b):
    b, s = x.shape[0], x.shape[1]
    o_diff = (rms_norm(o_diff, subln_g) * (1.0 - lam_init)).reshape(b, s, DIFF_WIDTH)
    z = jnp.concatenate([o_diff * jax.nn.silu(g_d),
                         conv_y * jax.nn.silu(g_c),
                         o_mem * jax.nn.silu(g_m)], axis=-1)
    o = jnp.einsum('bse,ed->bsd', z, w_out)
    return layer_norm(ALPHA * x + o, ln_g, ln_b)


def setup_inputs(seed: int = 0) -> dict:
    key = jax.random.key(seed)
    ks = jax.random.split(key, 20)
    f = jnp.float32

    def nrm(k, shape, s):
        return jax.random.normal(k, shape, f) * s

    return {
        "x_prompt": nrm(ks[0], (BATCH, SEQ, D_MODEL), 1.0),
        "x_sample": nrm(ks[1], (DEC_BATCH, DEC_SEQ, D_MODEL), 1.0),
        "cache_diff_k": nrm(ks[2], (DEPTH, DEC_BATCH, PAST_LEN, DIFF_H, 2 * DIFF_DH), 1.0),
        "cache_diff_v": nrm(ks[3], (DEPTH, DEC_BATCH, PAST_LEN, DIFF_H, 2 * DIFF_DH), 1.0),
        "cache_conv": nrm(ks[4], (DEPTH, DEC_BATCH, CONV_K - 1, CONV_W), 1.0),
        "cache_mem_k": nrm(ks[5], (DEPTH, DEC_BATCH, N_MEM, MEM_H, MEM_DH), 1.0),
        "cache_mem_v": nrm(ks[6], (DEPTH, DEC_BATCH, N_MEM, MEM_H, MEM_DH), 1.0),
        "mem_prompt": nrm(ks[7], (BATCH, N_MEM, D_MODEL), 1.0),
        "rel_bias_table": nrm(ks[8], (N_BUCKETS, DIFF_H), 0.5),
        "w_in": nrm(ks[9], (DEPTH, D_MODEL, PROJ_TOTAL), D_MODEL ** -0.5),
        "w_mem_kv": nrm(ks[10], (DEPTH, D_MODEL, 2 * MEM_WIDTH), D_MODEL ** -0.5),
        "conv_w": nrm(ks[11], (DEPTH, CONV_K, CONV_W), 0.5),
        "lambda_q1": nrm(ks[12], (DEPTH, DIFF_DH), 0.1),
        "lambda_k1": nrm(ks[13], (DEPTH, DIFF_DH), 0.1),
        "lambda_q2": nrm(ks[14], (DEPTH, DIFF_DH), 0.1),
        "lambda_k2": nrm(ks[15], (DEPTH, DIFF_DH), 0.1),
        "subln_g": 1.0 + nrm(ks[16], (DEPTH, 2 * DIFF_DH), 0.02),
        "w_out": nrm(ks[17], (DEPTH, MIX_WIDTH, D_MODEL), BETA * MIX_WIDTH ** -0.5),
        "ln_g": 1.0 + nrm(ks[18], (DEPTH, D_MODEL), 0.02),
        "ln_b": nrm(ks[19], (DEPTH, D_MODEL), 0.02),
    }


def reference(x_prompt, x_sample, cache_diff_k, cache_diff_v, cache_conv, cache_mem_k, cache_mem_v,
              mem_prompt, rel_bias_table, w_in, w_mem_kv, conv_w, lambda_q1, lambda_k1, lambda_q2,
              lambda_k2, subln_g, w_out, ln_g, ln_b):
    xp, xs = x_prompt, x_sample
    bp, sp = xp.shape[0], xp.shape[1]
    bs, ss = xs.shape[0], xs.shape[1]
    past = cache_diff_k.shape[2]
    kp_l, vp_l, cp_l, mkp_l, mvp_l, ks_l, vs_l, cs_l = [], [], [], [], [], [], [], []

    for l in range(DEPTH):
        lam_init = 0.8 - 0.6 * math.exp(-0.3 * l)
        lam = diff_lambda(lambda_q1[l], lambda_k1[l], lambda_q2[l], lambda_k2[l], lam_init)

        q, k, v, h, bg, cg, mq, g_d, g_c, g_m = in_projection(xp, w_in[l])
        o_diff = prompt_diff_attention(q, k, v, lam, rel_bias_table)
        u = cg * h
        u_pad = jnp.concatenate([jnp.zeros((bp, CONV_K - 1, CONV_W), u.dtype), u], axis=1)
        conv_y = bg * causal_conv(u_pad, conv_w[l], sp)
        mkv = jnp.einsum('bmd,de->bme', mem_prompt, w_mem_kv[l])
        mk, mv = jnp.split(mkv, 2, axis=-1)
        mk = mk.reshape(bp, N_MEM, MEM_H, MEM_DH)
        mv = mv.reshape(bp, N_MEM, MEM_H, MEM_DH)
        o_mem = mem_attention(mq, mk, mv)
        kp_l.append(k.reshape(bp, sp, DIFF_H, 2 * DIFF_DH))
        vp_l.append(v)
        cp_l.append(u[:, sp - (CONV_K - 1):])
        mkp_l.append(mk)
        mvp_l.append(mv)
        xp = merge_and_norm(xp, o_diff, conv_y, o_mem, g_d, g_c, g_m, subln_g[l], lam_init,
                            w_out[l], ln_g[l], ln_b[l])

        q, k, v, h, bg, cg, mq, g_d, g_c, g_m = in_projection(xs, w_in[l])
        k_all = jnp.concatenate([cache_diff_k[l].reshape(bs, past, DIFF_H, 2, DIFF_DH), k], axis=1)
        v_all = jnp.concatenate([cache_diff_v[l], v], axis=1)
        qpos = past + jnp.arange(ss)
        kpos = jnp.arange(past + ss)
        o_diff = diff_attention(q, k_all, v_all, rel_bias(qpos, kpos, rel_bias_table),
                                chunk_mask(qpos, kpos), lam)
        u = cg * h
        u_pad = jnp.concatenate([cache_conv[l], u], axis=1)
        conv_y = bg * causal_conv(u_pad, conv_w[l], ss)
        o_mem = mem_attention(mq, cache_mem_k[l], cache_mem_v[l])
        ks_l.append(k.reshape(bs, ss, DIFF_H, 2 * DIFF_DH))
        vs_l.append(v)
        cs_l.append(u_pad[:, ss:])
        xs = merge_and_norm(xs, o_diff, conv_y, o_mem, g_d, g_c, g_m, subln_g[l], lam_init,
                            w_out[l], ln_g[l], ln_b[l])

    return (xp, xs, jnp.stack(kp_l), jnp.stack(vp_l), jnp.stack(cp_l), jnp.stack(mkp_l),
            jnp.stack(mvp_l), jnp.stack(ks_l), jnp.stack(vs_l), jnp.stack(cs_l))
```

```python
import functools
import math

import jax
import jax.numpy as jnp
import numpy as np
from jax import lax
from jax.experimental import pallas as pl
from jax.experimental.pallas import tpu as pltpu

f32 = jnp.float32
bf16 = jnp.bfloat16

CHUNK = 64
DIFF_H = 8
DIFF_DH = 64
HEAD_W = 2 * DIFF_DH
MEM_H = 4
CONV_K = 3
N_BUCKETS = 32
MAX_DISTANCE = 128
EPS = 1e-5

LOG2E = 1.4426950408889634
NEG_BIG = -1e30
MASKED_BUCKET = N_BUCKETS

ROW_TILE = 512
ATTN_BLOCK = 256
VMEM_LIMIT = 56 * 1024 * 1024


def _params(n_axes):
    return pltpu.CompilerParams(dimension_semantics=("arbitrary",) * n_axes,
                                vmem_limit_bytes=VMEM_LIMIT)


def _silu(x):
    return x / (1.0 + jnp.exp(-x))


def _t5_bucket(rel):
    half = N_BUCKETS // 2
    max_exact = half // 2
    ret = jnp.where(rel > 0, half, 0)
    n = jnp.abs(rel)
    nf = jnp.maximum(n, 1).astype(f32)
    large = max_exact + (jnp.log(nf / max_exact) / math.log(MAX_DISTANCE / max_exact)
                         * (half - max_exact)).astype(jnp.int32)
    large = jnp.minimum(large, half - 1)
    return ret + jnp.where(n < max_exact, n, large)


def _bias_kernel(tab_ref, far_ref, bkt_ref, out_ref, *, subtract_far):
    h = pl.program_id(0)
    bkt = bkt_ref[...]
    val = jnp.zeros(bkt.shape, f32)
    for b in range(N_BUCKETS):
        val = jnp.where(bkt == b, tab_ref[b, h], val)
    if subtract_far:
        val = val - tab_ref[far_ref[0], h]
    out_ref[0] = jnp.where(bkt == MASKED_BUCKET, NEG_BIG, val * LOG2E)


def _bias_tiles(table, buckets, far_bucket, subtract_far):
    r, c = buckets.shape
    return pl.pallas_call(
        functools.partial(_bias_kernel, subtract_far=subtract_far),
        grid=(DIFF_H,),
        in_specs=[pl.BlockSpec(memory_space=pltpu.SMEM),
                  pl.BlockSpec(memory_space=pltpu.SMEM),
                  pl.BlockSpec((r, c), lambda h: (0, 0))],
        out_specs=pl.BlockSpec((1, r, c), lambda h: (h, 0, 0)),
        out_shape=jax.ShapeDtypeStruct((DIFF_H, r, c), f32),
        compiler_params=_params(1),
        name="bias_tiles",
    )(table, far_bucket, buckets)


def _prompt_buckets(blk):
    kk = jnp.arange(2 * blk)[:, None] - blk
    qq = jnp.arange(blk)[None, :]
    bkt = _t5_bucket(kk - qq)
    visible = (kk // CHUNK) <= (qq // CHUNK)
    return jnp.where(visible, bkt, MASKED_BUCKET).astype(jnp.int32)


def _sample_buckets(past, ss):
    qpos = past + jnp.arange(ss)[:, None]
    kpos = jnp.arange(past + ss)[None, :]
    bkt = _t5_bucket(kpos - qpos)
    visible = (kpos // CHUNK) <= (qpos // CHUNK)
    bkt = jnp.where(visible, bkt, MASKED_BUCKET).astype(jnp.int32)
    return jnp.concatenate([bkt, bkt], axis=0)


def _dot(a, b):
    return jnp.dot(a, b, preferred_element_type=f32)


def _proj_qk_kernel(x_ref, w_ref, q_ref, kb_ref, kf_ref, *, width, q_scale):
    xb = x_ref[...].astype(bf16)
    q_ref[...] = (_dot(xb, w_ref[:, :width]) * q_scale).astype(bf16)
    k = _dot(xb, w_ref[:, width:])
    kf_ref[...] = k
    kb_ref[...] = k.astype(bf16)


def _proj_v_kernel(x_ref, w_ref, vb_ref, vf_ref, sg_ref, *, width, transpose_v):
    xb = x_ref[...].astype(bf16)
    v = _dot(xb, w_ref[:, :width])
    vf_ref[...] = v
    if transpose_v:
        vb_ref[0] = v.T.astype(bf16)
    else:
        vb_ref[...] = v.astype(bf16)
    sg_ref[...] = _silu(_dot(xb, w_ref[:, width:])).astype(bf16)


def _proj_conv_kernel(x_ref, w_ref, cw_ref, init_ref, z_ref, u_ref, carry_ref, *, width, seq_len):
    rows_n = x_ref.shape[0]
    xb = x_ref[...].astype(bf16)
    h = _dot(xb, w_ref[:, 0 * width:1 * width])
    bg = _dot(xb, w_ref[:, 1 * width:2 * width])
    cg = _dot(xb, w_ref[:, 2 * width:3 * width])
    gc = _dot(xb, w_ref[:, 3 * width:4 * width])
    u = cg * h
    rows = lax.broadcasted_iota(jnp.int32, u.shape, 0)
    if seq_len >= rows_n:
        tiles_per_seq = seq_len // rows_n
        @pl.when(pl.program_id(0) % tiles_per_seq == 0)
        def _():
            carry_ref[...] = init_ref[...]

        carry = carry_ref[...]
        prev1 = jnp.where(rows == 0, carry[7:8], pltpu.roll(u, 1, 0))
        prev2 = jnp.where(rows == 0, carry[6:7], jnp.where(rows == 1, carry[7:8], pltpu.roll(u, 2, 0)))
        carry_ref[...] = u[rows_n - 8:]
        u_ref[0] = u[rows_n - 8:]
    else:
        t = rows % seq_len
        init = init_ref[...]
        prev1 = jnp.where(t == 0, pltpu.roll(init, rows_n - 1, 0), pltpu.roll(u, 1, 0))
        prev2 = jnp.where(t < 2, init, pltpu.roll(u, 2, 0))
        u_ref[...] = u
    cw = cw_ref[...]
    y = cw[0:1] * prev2 + cw[1:2] * prev1 + cw[2:3] * u
    z_ref[...] = ((bg * y) * _silu(gc)).astype(bf16)


def _proj_mem_kernel(x_ref, w_ref, mq_ref, sg_ref, *, width, q_scale):
    xb = x_ref[...].astype(bf16)
    mq_ref[...] = (_dot(xb, w_ref[:, :width]) * q_scale).astype(bf16)
    sg_ref[...] = _silu(_dot(xb, w_ref[:, width:])).astype(bf16)


def _row_spec(tm, n):
    return pl.BlockSpec((tm, n), lambda m: (m, 0))


def _full_spec(shape):
    return pl.BlockSpec(shape, lambda m: (0,) * len(shape))


def _proj_qk(x, w, tm):
    rows, d = x.shape
    width = w.shape[1] // 2
    sds = jax.ShapeDtypeStruct
    return pl.pallas_call(
        functools.partial(_proj_qk_kernel, width=width, q_scale=DIFF_DH ** -0.5 * LOG2E),
        grid=(rows // tm,),
        in_specs=[_row_spec(tm, d), _full_spec(w.shape)],
        out_specs=[_row_spec(tm, width)] * 3,
        out_shape=[sds((rows, width), bf16), sds((rows, width), bf16), sds((rows, width), f32)],
        compiler_params=_params(1),
        name="proj_qk",
    )(x, w)


def _proj_v(x, w, tm, n_seq, transpose_v):
    rows, d = x.shape
    width = w.shape[1] // 2
    seq = rows // n_seq
    sds = jax.ShapeDtypeStruct
    if transpose_v:
        tiles = seq // tm
        vb_spec = pl.BlockSpec((1, width, tm), lambda m: (m // tiles, 0, m % tiles))
        vb_shape = sds((n_seq, width, seq), bf16)
    else:
        vb_spec = _row_spec(tm, width)
        vb_shape = sds((rows, width), bf16)
    return pl.pallas_call(
        functools.partial(_proj_v_kernel, width=width, transpose_v=transpose_v),
        grid=(rows // tm,),
        in_specs=[_row_spec(tm, d), _full_spec(w.shape)],
        out_specs=[vb_spec, _row_spec(tm, width), _row_spec(tm, width)],
        out_shape=[vb_shape, sds((rows, width), f32), sds((rows, width), bf16)],
        compiler_params=_params(1),
        name="proj_v",
    )(x, w)


def _proj_conv(x, w, conv_w, init, tm, seq_len):
    rows, d = x.shape
    width = w.shape[1] // 4
    sds = jax.ShapeDtypeStruct
    if seq_len >= tm:
        tiles = seq_len // tm
        u_spec = pl.BlockSpec((1, 8, width), lambda m: (m // tiles, 0, 0))
        u_shape = sds((rows // seq_len, 8, width), f32)
        init_spec = _full_spec(init.shape)
    else:
        u_spec = _row_spec(tm, width)
        u_shape = sds((rows, width), f32)
        init_spec = _row_spec(tm, width)
    return pl.pallas_call(
        functools.partial(_proj_conv_kernel, width=width, seq_len=seq_len),
        grid=(rows // tm,),
        in_specs=[_row_spec(tm, d), _full_spec(w.shape), _full_spec(conv_w.shape), init_spec],
        out_specs=[_row_spec(tm, width), u_spec],
        out_shape=[sds((rows, width), bf16), u_shape],
        scratch_shapes=[pltpu.VMEM((8, width), f32)],
        compiler_params=_params(1),
        name="proj_conv",
    )(x, w, conv_w, init)


def _proj_mem(x, w, tm, head_dim):
    rows, d = x.shape
    width = w.shape[1] // 2
    sds = jax.ShapeDtypeStruct
    return pl.pallas_call(
        functools.partial(_proj_mem_kernel, width=width, q_scale=head_dim ** -0.5 * LOG2E),
        grid=(rows // tm,),
        in_specs=[_row_spec(tm, d), _full_spec(w.shape)],
        out_specs=[_row_spec(tm, width)] * 2,
        out_shape=[sds((rows, width), bf16)] * 2,
        compiler_params=_params(1),
        name="proj_mem",
    )(x, w)


def _memkv_kernel(x_ref, w_ref, k_ref, v_ref, *, width):
    xb = x_ref[...].astype(bf16)
    k_ref[...] = _dot(xb, w_ref[:, :width])
    v_ref[...] = _dot(xb, w_ref[:, width:])


def _memkv(x, w):
    rows, d = x.shape
    width = w.shape[1] // 2
    sds = jax.ShapeDtypeStruct
    return pl.pallas_call(
        functools.partial(_memkv_kernel, width=width),
        grid=(1,),
        in_specs=[_full_spec(x.shape), _full_spec(w.shape)],
        out_specs=[_full_spec((rows, width))] * 2,
        out_shape=[sds((rows, width), f32)] * 2,
        compiler_params=_params(1),
        name="mem_kv",
    )(x, w)


def _mem_attn_kernel(q_ref, sg_ref, k_ref, v_ref, z_ref, *, head_dim):
    for h in range(q_ref.shape[1] // head_dim):
        cols = slice(h * head_dim, (h + 1) * head_dim)
        kh = k_ref[0, :, cols].astype(bf16)
        vh = v_ref[0, :, cols].astype(bf16)
        s = lax.dot_general(q_ref[:, cols], kh, (((1,), (1,)), ((), ())), preferred_element_type=f32)
        p = jnp.exp2(s - jnp.max(s, axis=-1, keepdims=True))
        o = _dot(p.astype(bf16), vh) / jnp.sum(p, axis=-1, keepdims=True)
        z_ref[:, cols] = (o * sg_ref[:, cols].astype(f32)).astype(bf16)


def _mem_attn(q, sg, mk, mv, tm, head_dim):
    rows, width = q.shape
    n_seq, n_mem, _ = mk.shape
    tiles = (rows // n_seq) // tm
    kv_spec = pl.BlockSpec((1, n_mem, width), lambda m: (m // tiles, 0, 0))
    return pl.pallas_call(
        functools.partial(_mem_attn_kernel, head_dim=head_dim),
        grid=(rows // tm,),
        in_specs=[_row_spec(tm, width), _row_spec(tm, width), kv_spec, kv_spec],
        out_specs=_row_spec(tm, width),
        out_shape=jax.ShapeDtypeStruct((rows, width), bf16),
        compiler_params=_params(1),
        name="mem_attn",
    )(q, sg, mk, mv)


def _lambda(lamp_ref, lam_init):
    lp = lamp_ref[...]
    s1 = jnp.sum(lp[0:1] * lp[1:2], axis=1, keepdims=True)
    s2 = jnp.sum(lp[2:3] * lp[3:4], axis=1, keepdims=True)
    return jnp.exp(s1) - jnp.exp(s2) + lam_init


def _two_maps(q):
    lane = lax.broadcasted_iota(jnp.int32, q.shape, 1)
    zero = jnp.zeros_like(q)
    return jnp.concatenate([jnp.where(lane < DIFF_DH, q, zero), jnp.where(lane >= DIFF_DH, q, zero)], axis=0)


def _attn_prompt_kernel(q_ref, k_ref, vt_ref, sg_ref, bias_ref, lamp_ref, gcol_ref, o_ref,
                        m_ref, l_ref, acc_ref, *, blk, lam_init):
    i = pl.program_id(2)
    qcat = _two_maps(q_ref[0])
    m_ref[...] = jnp.full(m_ref.shape, NEG_BIG, f32)
    l_ref[...] = jnp.zeros(l_ref.shape, f32)
    acc_ref[...] = jnp.zeros(acc_ref.shape, f32)

    def step(j, bias):
        start = pl.multiple_of(j * blk, blk)
        kb = k_ref[0, pl.ds(start, blk), :]
        vtb = vt_ref[0, :, pl.ds(start, blk)]
        st = lax.dot_general(kb, qcat, (((1,), (1,)), ((), ())), preferred_element_type=f32)
        if bias is not None:
            st = st + jnp.concatenate([bias, bias], axis=1)
        m_old = m_ref[...]
        m_new = jnp.maximum(m_old, jnp.max(st, axis=0, keepdims=True))
        alpha = jnp.exp2(m_old - m_new)
        p = jnp.exp2(st - m_new)
        l_ref[...] = alpha * l_ref[...] + jnp.sum(p, axis=0, keepdims=True)
        acc_ref[...] = alpha * acc_ref[...] + _dot(vtb, p.astype(bf16))
        m_ref[...] = m_new

    def far_step(j, carry):
        step(j, None)
        return carry

    lax.fori_loop(0, jnp.maximum(i - 1, 0), far_step, 0)

    @pl.when(i > 0)
    def _():
        step(i - 1, bias_ref[0, :blk, :])

    step(i, bias_ref[0, blk:, :])

    lam = _lambda(lamp_ref, lam_init)
    ot = acc_ref[...] * (1.0 / l_ref[...])
    d = ot[:, :blk] - lam * ot[:, blk:]
    ms = jnp.mean(d * d, axis=0, keepdims=True)
    y = d * lax.rsqrt(ms + EPS) * gcol_ref[...]
    o_ref[0] = (y.T * sg_ref[0].astype(f32)).astype(bf16)


def _attn_prompt(q, k, vt, sg, bias, lamp, gcol, lam_init):
    b, s, w = q.shape
    blk = ATTN_BLOCK
    qspec = pl.BlockSpec((1, blk, HEAD_W), lambda bi, h, i: (bi, i, h))
    return pl.pallas_call(
        functools.partial(_attn_prompt_kernel, blk=blk, lam_init=lam_init),
        grid=(b, DIFF_H, s // blk),
        in_specs=[qspec,
                  pl.BlockSpec((1, s, HEAD_W), lambda bi, h, i: (bi, 0, h)),
                  pl.BlockSpec((1, HEAD_W, s), lambda bi, h, i: (bi, h, 0)),
                  qspec,
                  pl.BlockSpec((1, 2 * blk, blk), lambda bi, h, i: (h, 0, 0)),
                  pl.BlockSpec(lamp.shape, lambda bi, h, i: (0, 0)),
                  pl.BlockSpec(gcol.shape, lambda bi, h, i: (0, 0))],
        out_specs=qspec,
        out_shape=jax.ShapeDtypeStruct((b, s, w), bf16),
        scratch_shapes=[pltpu.VMEM((1, 2 * blk), f32), pltpu.VMEM((1, 2 * blk), f32),
                        pltpu.VMEM((HEAD_W, 2 * blk), f32)],
        compiler_params=_params(3),
        name="attn_prompt",
    )(q, k, vt, sg, bias, lamp, gcol)


def _attn_sample_kernel(q_ref, kn_ref, vn_ref, kc_ref, vc_ref, sg_ref, bias_ref, lamp_ref, grow_ref, o_ref,
                        *, past, lam_init):
    ss = q_ref.shape[1]
    lam = _lambda(lamp_ref, lam_init)
    nt = (((1,), (1,)), ((), ()))
    for h in range(DIFF_H):
        cols = slice(h * HEAD_W, (h + 1) * HEAD_W)
        qcat = _two_maps(q_ref[0, :, cols])
        sc = lax.dot_general(qcat, kc_ref[0, :, cols].astype(bf16), nt, preferred_element_type=f32)
        sn = lax.dot_general(qcat, kn_ref[0, :, cols], nt, preferred_element_type=f32)
        sc = sc + bias_ref[h, :, :past]
        sn = sn + bias_ref[h, :, past:]
        m = jnp.maximum(jnp.max(sc, axis=-1, keepdims=True), jnp.max(sn, axis=-1, keepdims=True))
        pc = jnp.exp2(sc - m)
        pn = jnp.exp2(sn - m)
        l = jnp.sum(pc, axis=-1, keepdims=True) + jnp.sum(pn, axis=-1, keepdims=True)
        o = (_dot(pc.astype(bf16), vc_ref[0, :, cols].astype(bf16)) + _dot(pn.astype(bf16), vn_ref[0, :, cols])) / l
        d = o[:ss] - lam * o[ss:]
        ms = jnp.mean(d * d, axis=-1, keepdims=True)
        y = d * lax.rsqrt(ms + EPS) * grow_ref[...]
        o_ref[0, :, cols] = (y * sg_ref[0, :, cols].astype(f32)).astype(bf16)


def _attn_sample(q, kn, vn, kc, vc, sg, bias, lamp, grow, lam_init):
    b, ss, w = q.shape
    past = kc.shape[1]
    new_spec = pl.BlockSpec((1, ss, w), lambda bi: (bi, 0, 0))
    cache_spec = pl.BlockSpec((1, past, w), lambda bi: (bi, 0, 0))
    return pl.pallas_call(
        functools.partial(_attn_sample_kernel, past=past, lam_init=lam_init),
        grid=(b,),
        in_specs=[new_spec, new_spec, new_spec, cache_spec, cache_spec, new_spec,
                  _full_spec(bias.shape), _full_spec(lamp.shape), _full_spec(grow.shape)],
        out_specs=new_spec,
        out_shape=jax.ShapeDtypeStruct((b, ss, w), bf16),
        compiler_params=_params(1),
        name="attn_sample",
    )(q, kn, vn, kc, vc, sg, bias, lamp, grow)


def _out_kernel(zd_ref, zc_ref, zm_ref, x_ref, w_ref, g_ref, b_ref, y_ref, *, alpha):
    wd = zd_ref.shape[1]
    wc = zc_ref.shape[1]
    o = (_dot(zd_ref[...], w_ref[:wd]) + _dot(zc_ref[...], w_ref[wd:wd + wc])
         + _dot(zm_ref[...], w_ref[wd + wc:]))
    r = alpha * x_ref[...] + o
    mu = jnp.mean(r, axis=-1, keepdims=True)
    c = r - mu
    var = jnp.mean(c * c, axis=-1, keepdims=True)
    y_ref[...] = c * lax.rsqrt(var + EPS) * g_ref[...] + b_ref[...]


def _out_proj(zd, zc, zm, x, w, g, b, tm, alpha):
    rows, d = x.shape
    return pl.pallas_call(
        functools.partial(_out_kernel, alpha=alpha),
        grid=(rows // tm,),
        in_specs=[_row_spec(tm, zd.shape[1]), _row_spec(tm, zc.shape[1]), _row_spec(tm, zm.shape[1]),
                  _row_spec(tm, d), _full_spec(w.shape), _full_spec(g.shape), _full_spec(b.shape)],
        out_specs=_row_spec(tm, d),
        out_shape=jax.ShapeDtypeStruct((rows, d), f32),
        compiler_params=_params(1),
        name="out_proj",
    )(zd, zc, zm, x, w, g, b)


def kernel(x_prompt, x_sample, cache_diff_k, cache_diff_v, cache_conv, cache_mem_k, cache_mem_v, mem_prompt,
           rel_bias_table, w_in, w_mem_kv, conv_w, lambda_q1, lambda_k1, lambda_q2, lambda_k2, subln_g, w_out,
           ln_g, ln_b):
    depth = w_in.shape[0]
    bp, sp, d_model = x_prompt.shape
    bs, ss, _ = x_sample.shape
    past = cache_diff_k.shape[2]
    n_mem = mem_prompt.shape[1]
    diff_w = DIFF_H * HEAD_W
    conv_ch = conv_w.shape[2]
    mem_w = cache_mem_k.shape[3] * cache_mem_k.shape[4]
    mem_dh = cache_mem_k.shape[4]
    alpha = (2 * depth) ** 0.25
    blk = ATTN_BLOCK
    assert sp % blk == 0 and blk % CHUNK == 0 and sp % ROW_TILE == 0 and (bs * ss) % ROW_TILE == 0
    assert conv_w.shape[1] == CONV_K and ss >= CONV_K - 1

    widths = (diff_w, diff_w, diff_w, conv_ch, conv_ch, conv_ch, mem_w, diff_w, conv_ch, mem_w)
    offs = np.concatenate([[0], np.cumsum(widths)])
    c_q, c_k, c_v, c_h, c_bg, c_cg, c_mq, c_gd, c_gc, c_gm = [slice(int(a), int(b)) for a, b in zip(offs[:-1], offs[1:])]

    far_bucket = _t5_bucket(jnp.array([-(blk + 1)], jnp.int32)).astype(jnp.int32)
    bias_p = _bias_tiles(rel_bias_table, _prompt_buckets(blk), far_bucket, True)
    bias_s = _bias_tiles(rel_bias_table, _sample_buckets(past, ss), far_bucket, False)

    xp = x_prompt.reshape(bp * sp, d_model)
    xs = x_sample.reshape(bs * ss, d_model)
    outs = {k: [] for k in ("kp", "vp", "cp", "mkp", "mvp", "ks", "vs", "cs")}

    for l in range(depth):
        lam_init = 0.8 - 0.6 * math.exp(-0.3 * l)
        wl = w_in[l]
        w_qk = jnp.concatenate([wl[:, c_q], wl[:, c_k]], axis=1).astype(bf16)
        w_v = jnp.concatenate([wl[:, c_v], wl[:, c_gd]], axis=1).astype(bf16)
        w_cv = jnp.concatenate([wl[:, c_h], wl[:, c_bg], wl[:, c_cg], wl[:, c_gc]], axis=1).astype(bf16)
        w_mq = jnp.concatenate([wl[:, c_mq], wl[:, c_gm]], axis=1).astype(bf16)
        w_o = w_out[l].astype(bf16)
        w_kv = w_mem_kv[l].astype(bf16)
        lamp = jnp.stack([lambda_q1[l], lambda_k1[l], lambda_q2[l], lambda_k2[l]]).astype(f32)
        gscaled = subln_g[l].astype(f32) * (1.0 - lam_init)
        gcol = jnp.broadcast_to(gscaled[:, None], (HEAD_W, blk))
        grow = gscaled[None, :]
        g_ln = ln_g[l][None, :].astype(f32)
        b_ln = ln_b[l][None, :].astype(f32)
        cw = conv_w[l].astype(f32)

        q, kb, kf = _proj_qk(xp, w_qk, ROW_TILE)
        vt, vf, sgd = _proj_v(xp, w_v, ROW_TILE, bp, True)
        zc, utail = _proj_conv(xp, w_cv, cw, jnp.zeros((8, conv_ch), f32), ROW_TILE, sp)
        mq, sgm = _proj_mem(xp, w_mq, ROW_TILE, mem_dh)
        mk, mv = _memkv(mem_prompt.reshape(bp * n_mem, d_model), w_kv)
        zm = _mem_attn(mq, sgm, mk.reshape(bp, n_mem, mem_w), mv.reshape(bp, n_mem, mem_w), ROW_TILE, mem_dh)
        zd = _attn_prompt(q.reshape(bp, sp, diff_w), kb.reshape(bp, sp, diff_w), vt,
                          sgd.reshape(bp, sp, diff_w), bias_p, lamp, gcol, lam_init)
        outs["kp"].append(kf.reshape(bp, sp, DIFF_H, HEAD_W))
        outs["vp"].append(vf.reshape(bp, sp, DIFF_H, HEAD_W))
        outs["cp"].append(utail[:, 8 - (CONV_K - 1):, :])
        outs["mkp"].append(mk.reshape(bp, n_mem, MEM_H, mem_dh))
        outs["mvp"].append(mv.reshape(bp, n_mem, MEM_H, mem_dh))
        xp = _out_proj(zd.reshape(bp * sp, diff_w), zc, zm, xp, w_o, g_ln, b_ln, ROW_TILE, alpha)

        q, kb, kf = _proj_qk(xs, w_qk, ROW_TILE)
        vb, vf, sgd = _proj_v(xs, w_v, ROW_TILE, bs, False)
        conv_init = jnp.pad(cache_conv[l].astype(f32), ((0, 0), (0, ss - (CONV_K - 1)), (0, 0)))
        zc, u = _proj_conv(xs, w_cv, cw, conv_init.reshape(bs * ss, conv_ch), ROW_TILE, ss)
        mq, sgm = _proj_mem(xs, w_mq, ROW_TILE, mem_dh)
        zm = _mem_attn(mq, sgm, cache_mem_k[l].reshape(bs, n_mem, mem_w), cache_mem_v[l].reshape(bs, n_mem, mem_w),
                       ss, mem_dh)
        zd = _attn_sample(q.reshape(bs, ss, diff_w), kb.reshape(bs, ss, diff_w), vb.reshape(bs, ss, diff_w),
                          cache_diff_k[l].reshape(bs, past, diff_w), cache_diff_v[l].reshape(bs, past, diff_w),
                          sgd.reshape(bs, ss, diff_w), bias_s, lamp, grow, lam_init)
        outs["ks"].append(kf.reshape(bs, ss, DIFF_H, HEAD_W))
        outs["vs"].append(vf.reshape(bs, ss, DIFF_H, HEAD_W))
        outs["cs"].append(u.reshape(bs, ss, conv_ch)[:, ss - (CONV_K - 1):, :])
        xs = _out_proj(zd.reshape(bs * ss, diff_w), zc, zm, xs, w_o, g_ln, b_ln, ROW_TILE, alpha)

    return (xp.reshape(bp, sp, d_model), xs.reshape(bs, ss, d_model),
            jnp.stack(outs["kp"]), jnp.stack(outs["vp"]), jnp.stack(outs["cp"]), jnp.stack(outs["mkp"]),
            jnp.stack(outs["mvp"]), jnp.stack(outs["ks"]), jnp.stack(outs["vs"]), jnp.stack(outs["cs"]))
```

```python
import functools
import math

import jax
import jax.numpy as jnp
import numpy as np
from jax import lax
from jax.experimental import pallas as pl
from jax.experimental.pallas import tpu as pltpu

f32 = jnp.float32
bf16 = jnp.bfloat16

CHUNK = 64
DIFF_H = 8
DIFF_DH = 64
HEAD_W = 2 * DIFF_DH
MEM_H = 4
CONV_K = 3
N_BUCKETS = 32
MAX_DISTANCE = 128
EPS = 1e-5

LOG2E = 1.4426950408889634
NEG_BIG = -1e30
MASKED_BUCKET = N_BUCKETS

ROW_TILE = 512
ATTN_BLOCK = 512
SUM_ROWS = 16
VMEM_LIMIT = 56 * 1024 * 1024


def _params(n_axes):
    return pltpu.CompilerParams(dimension_semantics=("arbitrary",) * n_axes,
                                vmem_limit_bytes=VMEM_LIMIT)


def _silu(x):
    return x / (1.0 + jnp.exp(-x))


def _t5_bucket(rel):
    half = N_BUCKETS // 2
    max_exact = half // 2
    ret = jnp.where(rel > 0, half, 0)
    n = jnp.abs(rel)
    nf = jnp.maximum(n, 1).astype(f32)
    large = max_exact + (jnp.log(nf / max_exact) / math.log(MAX_DISTANCE / max_exact)
                         * (half - max_exact)).astype(jnp.int32)
    large = jnp.minimum(large, half - 1)
    return ret + jnp.where(n < max_exact, n, large)


def _bias_kernel(tab_ref, far_ref, bkt_ref, out_ref, *, subtract_far):
    h = pl.program_id(0)
    bkt = bkt_ref[...]
    val = jnp.zeros(bkt.shape, f32)
    for b in range(N_BUCKETS):
        val = jnp.where(bkt == b, tab_ref[b, h], val)
    if subtract_far:
        val = val - tab_ref[far_ref[0], h]
    out_ref[0] = jnp.where(bkt == MASKED_BUCKET, NEG_BIG, val * LOG2E)


def _bias_tiles(table, buckets, far_bucket, subtract_far):
    r, c = buckets.shape
    return pl.pallas_call(
        functools.partial(_bias_kernel, subtract_far=subtract_far),
        grid=(DIFF_H,),
        in_specs=[pl.BlockSpec(memory_space=pltpu.SMEM),
                  pl.BlockSpec(memory_space=pltpu.SMEM),
                  pl.BlockSpec((r, c), lambda h: (0, 0))],
        out_specs=pl.BlockSpec((1, r, c), lambda h: (h, 0, 0)),
        out_shape=jax.ShapeDtypeStruct((DIFF_H, r, c), f32),
        compiler_params=_params(1),
        name="bias_tiles",
    )(table, far_bucket, buckets)


def _prompt_buckets(blk):
    kk = jnp.arange(2 * blk)[:, None] - blk
    qq = jnp.arange(blk)[None, :]
    bkt = _t5_bucket(kk - qq)
    visible = (kk // CHUNK) <= (qq // CHUNK)
    return jnp.where(visible, bkt, MASKED_BUCKET).astype(jnp.int32)


def _sample_buckets(past, ss):
    qpos = past + jnp.arange(ss)[:, None]
    kpos = jnp.arange(past + ss)[None, :]
    bkt = _t5_bucket(kpos - qpos)
    visible = (kpos // CHUNK) <= (qpos // CHUNK)
    bkt = jnp.where(visible, bkt, MASKED_BUCKET).astype(jnp.int32)
    return jnp.concatenate([bkt, bkt], axis=0)


def _dot(a, b):
    return jnp.dot(a, b, preferred_element_type=f32)


def _proj_qk_kernel(x_ref, w_ref, q_ref, kb_ref, kf_ref, *, width, q_scale):
    xb = x_ref[...].astype(bf16)
    q_ref[...] = (_dot(xb, w_ref[:, :width]) * q_scale).astype(bf16)
    k = _dot(xb, w_ref[:, width:])
    kf_ref[...] = k
    kb_ref[...] = k.astype(bf16)


def _proj_v_kernel(x_ref, w_ref, vb_ref, vf_ref, sg_ref, *, width, transpose_v):
    xb = x_ref[...].astype(bf16)
    v = _dot(xb, w_ref[:, :width])
    vf_ref[...] = v
    if transpose_v:
        vb_ref[0] = v.T.astype(bf16)
    else:
        vb_ref[...] = v.astype(bf16)
    sg_ref[...] = _silu(_dot(xb, w_ref[:, width:])).astype(bf16)


def _proj_conv_kernel(x_ref, w_ref, cw_ref, init_ref, z_ref, u_ref, carry_ref, *, width, seq_len):
    rows_n = x_ref.shape[0]
    xb = x_ref[...].astype(bf16)
    h = _dot(xb, w_ref[:, 0 * width:1 * width])
    bg = _dot(xb, w_ref[:, 1 * width:2 * width])
    cg = _dot(xb, w_ref[:, 2 * width:3 * width])
    gc = _dot(xb, w_ref[:, 3 * width:4 * width])
    u = cg * h
    rows = lax.broadcasted_iota(jnp.int32, u.shape, 0)
    if seq_len >= rows_n:
        tiles_per_seq = seq_len // rows_n
        @pl.when(pl.program_id(0) % tiles_per_seq == 0)
        def _():
            carry_ref[...] = init_ref[...]

        carry = carry_ref[...]
        prev1 = jnp.where(rows == 0, carry[7:8], pltpu.roll(u, 1, 0))
        prev2 = jnp.where(rows == 0, carry[6:7], jnp.where(rows == 1, carry[7:8], pltpu.roll(u, 2, 0)))
        carry_ref[...] = u[rows_n - 8:]
        u_ref[0] = u[rows_n - 8:]
    else:
        t = rows % seq_len
        init = init_ref[...]
        prev1 = jnp.where(t == 0, pltpu.roll(init, rows_n - 1, 0), pltpu.roll(u, 1, 0))
        prev2 = jnp.where(t < 2, init, pltpu.roll(u, 2, 0))
        u_ref[...] = u
    cw = cw_ref[...]
    y = cw[0:1] * prev2 + cw[1:2] * prev1 + cw[2:3] * u
    z_ref[...] = ((bg * y) * _silu(gc)).astype(bf16)


def _proj_mem_kernel(x_ref, w_ref, mq_ref, sg_ref, *, width, q_scale):
    xb = x_ref[...].astype(bf16)
    mq_ref[...] = (_dot(xb, w_ref[:, :width]) * q_scale).astype(bf16)
    sg_ref[...] = _silu(_dot(xb, w_ref[:, width:])).astype(bf16)


def _row_spec(tm, n):
    return pl.BlockSpec((tm, n), lambda m: (m, 0))


def _full_spec(shape):
    return pl.BlockSpec(shape, lambda m: (0,) * len(shape))


def _proj_qk(x, w, tm):
    rows, d = x.shape
    width = w.shape[1] // 2
    sds = jax.ShapeDtypeStruct
    return pl.pallas_call(
        functools.partial(_proj_qk_kernel, width=width, q_scale=DIFF_DH ** -0.5 * LOG2E),
        grid=(rows // tm,),
        in_specs=[_row_spec(tm, d), _full_spec(w.shape)],
        out_specs=[_row_spec(tm, width)] * 3,
        out_shape=[sds((rows, width), bf16), sds((rows, width), bf16), sds((rows, width), f32)],
        compiler_params=_params(1),
        name="proj_qk",
    )(x, w)


def _proj_v(x, w, tm, n_seq, transpose_v):
    rows, d = x.shape
    width = w.shape[1] // 2
    seq = rows // n_seq
    sds = jax.ShapeDtypeStruct
    if transpose_v:
        tiles = seq // tm
        vb_spec = pl.BlockSpec((1, width, tm), lambda m: (m // tiles, 0, m % tiles))
        vb_shape = sds((n_seq, width, seq), bf16)
    else:
        vb_spec = _row_spec(tm, width)
        vb_shape = sds((rows, width), bf16)
    return pl.pallas_call(
        functools.partial(_proj_v_kernel, width=width, transpose_v=transpose_v),
        grid=(rows // tm,),
        in_specs=[_row_spec(tm, d), _full_spec(w.shape)],
        out_specs=[vb_spec, _row_spec(tm, width), _row_spec(tm, width)],
        out_shape=[vb_shape, sds((rows, width), f32), sds((rows, width), bf16)],
        compiler_params=_params(1),
        name="proj_v",
    )(x, w)


def _proj_conv(x, w, conv_w, init, tm, seq_len):
    rows, d = x.shape
    width = w.shape[1] // 4
    sds = jax.ShapeDtypeStruct
    if seq_len >= tm:
        tiles = seq_len // tm
        u_spec = pl.BlockSpec((1, 8, width), lambda m: (m // tiles, 0, 0))
        u_shape = sds((rows // seq_len, 8, width), f32)
        init_spec = _full_spec(init.shape)
    else:
        u_spec = _row_spec(tm, width)
        u_shape = sds((rows, width), f32)
        init_spec = _row_spec(tm, width)
    return pl.pallas_call(
        functools.partial(_proj_conv_kernel, width=width, seq_len=seq_len),
        grid=(rows // tm,),
        in_specs=[_row_spec(tm, d), _full_spec(w.shape), _full_spec(conv_w.shape), init_spec],
        out_specs=[_row_spec(tm, width), u_spec],
        out_shape=[sds((rows, width), bf16), u_shape],
        scratch_shapes=[pltpu.VMEM((8, width), f32)],
        compiler_params=_params(1),
        name="proj_conv",
    )(x, w, conv_w, init)


def _proj_mem(x, w, tm, head_dim):
    rows, d = x.shape
    width = w.shape[1] // 2
    sds = jax.ShapeDtypeStruct
    return pl.pallas_call(
        functools.partial(_proj_mem_kernel, width=width, q_scale=head_dim ** -0.5 * LOG2E),
        grid=(rows // tm,),
        in_specs=[_row_spec(tm, d), _full_spec(w.shape)],
        out_specs=[_row_spec(tm, width)] * 2,
        out_shape=[sds((rows, width), bf16)] * 2,
        compiler_params=_params(1),
        name="proj_mem",
    )(x, w)


def _memkv_kernel(x_ref, w_ref, k_ref, v_ref, *, width):
    xb = x_ref[...].astype(bf16)
    k_ref[...] = _dot(xb, w_ref[:, :width])
    v_ref[...] = _dot(xb, w_ref[:, width:])


def _memkv(x, w):
    rows, d = x.shape
    width = w.shape[1] // 2
    sds = jax.ShapeDtypeStruct
    return pl.pallas_call(
        functools.partial(_memkv_kernel, width=width),
        grid=(1,),
        in_specs=[_full_spec(x.shape), _full_spec(w.shape)],
        out_specs=[_full_spec((rows, width))] * 2,
        out_shape=[sds((rows, width), f32)] * 2,
        compiler_params=_params(1),
        name="mem_kv",
    )(x, w)


def _mem_attn_kernel(q_ref, sg_ref, k_ref, v_ref, z_ref, *, head_dim):
    for h in range(q_ref.shape[1] // head_dim):
        cols = slice(h * head_dim, (h + 1) * head_dim)
        kh = k_ref[0, :, cols].astype(bf16)
        vh = v_ref[0, :, cols].astype(bf16)
        s = lax.dot_general(q_ref[:, cols], kh, (((1,), (1,)), ((), ())), preferred_element_type=f32)
        p = jnp.exp2(s - jnp.max(s, axis=-1, keepdims=True))
        o = _dot(p.astype(bf16), vh) / jnp.sum(p, axis=-1, keepdims=True)
        z_ref[:, cols] = (o * sg_ref[:, cols].astype(f32)).astype(bf16)


def _mem_attn(q, sg, mk, mv, tm, head_dim):
    rows, width = q.shape
    n_seq, n_mem, _ = mk.shape
    tiles = (rows // n_seq) // tm
    kv_spec = pl.BlockSpec((1, n_mem, width), lambda m: (m // tiles, 0, 0))
    return pl.pallas_call(
        functools.partial(_mem_attn_kernel, head_dim=head_dim),
        grid=(rows // tm,),
        in_specs=[_row_spec(tm, width), _row_spec(tm, width), kv_spec, kv_spec],
        out_specs=_row_spec(tm, width),
        out_shape=jax.ShapeDtypeStruct((rows, width), bf16),
        compiler_params=_params(1),
        name="mem_attn",
    )(q, sg, mk, mv)


def _lambda(lamp_ref, lam_init):
    lp = lamp_ref[...]
    s1 = jnp.sum(lp[0:1] * lp[1:2], axis=1, keepdims=True)
    s2 = jnp.sum(lp[2:3] * lp[3:4], axis=1, keepdims=True)
    return jnp.exp(s1) - jnp.exp(s2) + lam_init


def _two_maps(q):
    lane = lax.broadcasted_iota(jnp.int32, q.shape, 1)
    zero = jnp.zeros_like(q)
    return jnp.concatenate([jnp.where(lane < DIFF_DH, q, zero), jnp.where(lane >= DIFF_DH, q, zero)], axis=0)


def _attn_prompt_kernel(q_ref, k_ref, vt_ref, sg_ref, bias_ref, lamp_ref, gcol_ref, o_ref,
                        m_ref, mx_ref, alpha_ref, acc_ref, s_ref, p_ref, *, blk, lam_init):
    i = pl.program_id(2)
    qcat = _two_maps(q_ref[0])
    ones = jnp.ones((SUM_ROWS, blk), bf16)
    m_ref[...] = jnp.full(m_ref.shape, NEG_BIG, f32)
    acc_ref[...] = jnp.zeros(acc_ref.shape, f32)
    p_ref[...] = jnp.zeros(p_ref.shape, bf16)
    alpha_ref[...] = jnp.ones(alpha_ref.shape, f32)

    def put_scores(j, bias):
        kb = k_ref[0, pl.ds(pl.multiple_of(j * blk, blk), blk), :]
        st = lax.dot_general(kb, qcat, (((1,), (1,)), ((), ())), preferred_element_type=f32)
        if bias is not None:
            st = st + jnp.concatenate([bias, bias], axis=1)
        s_ref[...] = st
        mx_ref[...] = jnp.max(st, axis=0, keepdims=True)

    def weighted_values(j):
        vtb = jnp.concatenate([vt_ref[0, :, pl.ds(pl.multiple_of(j * blk, blk), blk)], ones], axis=0)
        return _dot(vtb, p_ref[...])[:acc_ref.shape[0]]

    def step(j, next_bias, prefetch):
        pv_prev = weighted_values(jnp.maximum(j - 1, 0))
        m_old = m_ref[...]
        m_new = jnp.maximum(m_old, mx_ref[...])
        p = jnp.exp2(s_ref[...] - m_new).astype(bf16)
        if prefetch:
            put_scores(j + 1, next_bias)
        acc_ref[...] = alpha_ref[...] * acc_ref[...] + pv_prev
        alpha_ref[...] = jnp.exp2(m_old - m_new)
        m_ref[...] = m_new
        p_ref[...] = p

    bias_prev = lambda: bias_ref[0, :blk, :]
    bias_diag = lambda: bias_ref[0, blk:, :]

    @pl.when(i == 0)
    def _():
        put_scores(0, bias_diag())

    @pl.when(i == 1)
    def _():
        put_scores(0, bias_prev())

    @pl.when(i >= 2)
    def _():
        put_scores(0, None)

    def far_step(j, carry):
        step(j, None, True)
        return carry

    lax.fori_loop(0, jnp.maximum(i - 2, 0), far_step, 0)

    @pl.when(i >= 2)
    def _():
        step(i - 2, bias_prev(), True)

    @pl.when(i >= 1)
    def _():
        step(i - 1, bias_diag(), True)

    step(i, None, False)
    acc_ref[...] = alpha_ref[...] * acc_ref[...] + weighted_values(i)

    lam = _lambda(lamp_ref, lam_init)
    ot = acc_ref[:HEAD_W, :] * (1.0 / acc_ref[HEAD_W:HEAD_W + 1, :])
    d = ot[:, :blk] - lam * ot[:, blk:]
    ms = jnp.mean(d * d, axis=0, keepdims=True)
    y = d * lax.rsqrt(ms + EPS) * gcol_ref[...]
    o_ref[0] = (y.T * sg_ref[0].astype(f32)).astype(bf16)


def _attn_prompt(q, k, vt, sg, bias, lamp, gcol, lam_init):
    b, s, w = q.shape
    blk = ATTN_BLOCK
    qspec = pl.BlockSpec((1, blk, HEAD_W), lambda bi, h, i: (bi, i, h))
    return pl.pallas_call(
        functools.partial(_attn_prompt_kernel, blk=blk, lam_init=lam_init),
        grid=(b, DIFF_H, s // blk),
        in_specs=[qspec,
                  pl.BlockSpec((1, s, HEAD_W), lambda bi, h, i: (bi, 0, h)),
                  pl.BlockSpec((1, HEAD_W, s), lambda bi, h, i: (bi, h, 0)),
                  qspec,
                  pl.BlockSpec((1, 2 * blk, blk), lambda bi, h, i: (h, 0, 0)),
                  pl.BlockSpec(lamp.shape, lambda bi, h, i: (0, 0)),
                  pl.BlockSpec(gcol.shape, lambda bi, h, i: (0, 0))],
        out_specs=qspec,
        out_shape=jax.ShapeDtypeStruct((b, s, w), bf16),
        scratch_shapes=[pltpu.VMEM((1, 2 * blk), f32)] * 3 + [
                        pltpu.VMEM((HEAD_W + 8, 2 * blk), f32),
                        pltpu.VMEM((blk, 2 * blk), f32), pltpu.VMEM((blk, 2 * blk), bf16)],
        compiler_params=_params(3),
        name="attn_prompt",
    )(q, k, vt, sg, bias, lamp, gcol)


def _attn_sample_kernel(q_ref, kn_ref, vn_ref, kc_ref, vc_ref, sg_ref, bias_ref, lamp_ref, grow_ref, o_ref,
                        *, past, lam_init):
    ss = q_ref.shape[1]
    lam = _lambda(lamp_ref, lam_init)
    nt = (((1,), (1,)), ((), ()))
    for h in range(DIFF_H):
        cols = slice(h * HEAD_W, (h + 1) * HEAD_W)
        qcat = _two_maps(q_ref[0, :, cols])
        sc = lax.dot_general(qcat, kc_ref[0, :, cols].astype(bf16), nt, preferred_element_type=f32)
        sn = lax.dot_general(qcat, kn_ref[0, :, cols], nt, preferred_element_type=f32)
        sc = sc + bias_ref[h, :, :past]
        sn = sn + bias_ref[h, :, past:]
        m = jnp.maximum(jnp.max(sc, axis=-1, keepdims=True), jnp.max(sn, axis=-1, keepdims=True))
        pc = jnp.exp2(sc - m)
        pn = jnp.exp2(sn - m)
        l = jnp.sum(pc, axis=-1, keepdims=True) + jnp.sum(pn, axis=-1, keepdims=True)
        o = (_dot(pc.astype(bf16), vc_ref[0, :, cols].astype(bf16)) + _dot(pn.astype(bf16), vn_ref[0, :, cols])) / l
        d = o[:ss] - lam * o[ss:]
        ms = jnp.mean(d * d, axis=-1, keepdims=True)
        y = d * lax.rsqrt(ms + EPS) * grow_ref[...]
        o_ref[0, :, cols] = (y * sg_ref[0, :, cols].astype(f32)).astype(bf16)


def _attn_sample(q, kn, vn, kc, vc, sg, bias, lamp, grow, lam_init):
    b, ss, w = q.shape
    past = kc.shape[1]
    new_spec = pl.BlockSpec((1, ss, w), lambda bi: (bi, 0, 0))
    cache_spec = pl.BlockSpec((1, past, w), lambda bi: (bi, 0, 0))
    return pl.pallas_call(
        functools.partial(_attn_sample_kernel, past=past, lam_init=lam_init),
        grid=(b,),
        in_specs=[new_spec, new_spec, new_spec, cache_spec, cache_spec, new_spec,
                  _full_spec(bias.shape), _full_spec(lamp.shape), _full_spec(grow.shape)],
        out_specs=new_spec,
        out_shape=jax.ShapeDtypeStruct((b, ss, w), bf16),
        compiler_params=_params(1),
        name="attn_sample",
    )(q, kn, vn, kc, vc, sg, bias, lamp, grow)


def _out_kernel(zd_ref, zc_ref, zm_ref, x_ref, w_ref, g_ref, b_ref, y_ref, *, alpha):
    wd = zd_ref.shape[1]
    wc = zc_ref.shape[1]
    o = (_dot(zd_ref[...], w_ref[:wd]) + _dot(zc_ref[...], w_ref[wd:wd + wc])
         + _dot(zm_ref[...], w_ref[wd + wc:]))
    r = alpha * x_ref[...] + o
    mu = jnp.mean(r, axis=-1, keepdims=True)
    c = r - mu
    var = jnp.mean(c * c, axis=-1, keepdims=True)
    y_ref[...] = c * lax.rsqrt(var + EPS) * g_ref[...] + b_ref[...]


def _out_proj(zd, zc, zm, x, w, g, b, tm, alpha):
    rows, d = x.shape
    return pl.pallas_call(
        functools.partial(_out_kernel, alpha=alpha),
        grid=(rows // tm,),
        in_specs=[_row_spec(tm, zd.shape[1]), _row_spec(tm, zc.shape[1]), _row_spec(tm, zm.shape[1]),
                  _row_spec(tm, d), _full_spec(w.shape), _full_spec(g.shape), _full_spec(b.shape)],
        out_specs=_row_spec(tm, d),
        out_shape=jax.ShapeDtypeStruct((rows, d), f32),
        compiler_params=_params(1),
        name="out_proj",
    )(zd, zc, zm, x, w, g, b)


def kernel(x_prompt, x_sample, cache_diff_k, cache_diff_v, cache_conv, cache_mem_k, cache_mem_v, mem_prompt,
           rel_bias_table, w_in, w_mem_kv, conv_w, lambda_q1, lambda_k1, lambda_q2, lambda_k2, subln_g, w_out,
           ln_g, ln_b):
    depth = w_in.shape[0]
    bp, sp, d_model = x_prompt.shape
    bs, ss, _ = x_sample.shape
    past = cache_diff_k.shape[2]
    n_mem = mem_prompt.shape[1]
    diff_w = DIFF_H * HEAD_W
    conv_ch = conv_w.shape[2]
    mem_w = cache_mem_k.shape[3] * cache_mem_k.shape[4]
    mem_dh = cache_mem_k.shape[4]
    alpha = (2 * depth) ** 0.25
    blk = ATTN_BLOCK
    assert sp % blk == 0 and blk % CHUNK == 0 and sp % ROW_TILE == 0 and (bs * ss) % ROW_TILE == 0
    assert conv_w.shape[1] == CONV_K and ss >= CONV_K - 1

    widths = (diff_w, diff_w, diff_w, conv_ch, conv_ch, conv_ch, mem_w, diff_w, conv_ch, mem_w)
    offs = np.concatenate([[0], np.cumsum(widths)])
    c_q, c_k, c_v, c_h, c_bg, c_cg, c_mq, c_gd, c_gc, c_gm = [slice(int(a), int(b)) for a, b in zip(offs[:-1], offs[1:])]

    far_bucket = _t5_bucket(jnp.array([-(blk + 1)], jnp.int32)).astype(jnp.int32)
    bias_p = _bias_tiles(rel_bias_table, _prompt_buckets(blk), far_bucket, True)
    bias_s = _bias_tiles(rel_bias_table, _sample_buckets(past, ss), far_bucket, False)

    xp = x_prompt.reshape(bp * sp, d_model)
    xs = x_sample.reshape(bs * ss, d_model)
    outs = {k: [] for k in ("kp", "vp", "cp", "mkp", "mvp", "ks", "vs", "cs")}

    for l in range(depth):
        lam_init = 0.8 - 0.6 * math.exp(-0.3 * l)
        wl = w_in[l]
        w_qk = jnp.concatenate([wl[:, c_q], wl[:, c_k]], axis=1).astype(bf16)
        w_v = jnp.concatenate([wl[:, c_v], wl[:, c_gd]], axis=1).astype(bf16)
        w_cv = jnp.concatenate([wl[:, c_h], wl[:, c_bg], wl[:, c_cg], wl[:, c_gc]], axis=1).astype(bf16)
        w_mq = jnp.concatenate([wl[:, c_mq], wl[:, c_gm]], axis=1).astype(bf16)
        w_o = w_out[l].astype(bf16)
        w_kv = w_mem_kv[l].astype(bf16)
        lamp = jnp.stack([lambda_q1[l], lambda_k1[l], lambda_q2[l], lambda_k2[l]]).astype(f32)
        gscaled = subln_g[l].astype(f32) * (1.0 - lam_init)
        gcol = jnp.broadcast_to(gscaled[:, None], (HEAD_W, blk))
        grow = gscaled[None, :]
        g_ln = ln_g[l][None, :].astype(f32)
        b_ln = ln_b[l][None, :].astype(f32)
        cw = conv_w[l].astype(f32)

        q, kb, kf = _proj_qk(xp, w_qk, ROW_TILE)
        vt, vf, sgd = _proj_v(xp, w_v, ROW_TILE, bp, True)
        zc, utail = _proj_conv(xp, w_cv, cw, jnp.zeros((8, conv_ch), f32), ROW_TILE, sp)
        mq, sgm = _proj_mem(xp, w_mq, ROW_TILE, mem_dh)
        mk, mv = _memkv(mem_prompt.reshape(bp * n_mem, d_model), w_kv)
        zm = _mem_attn(mq, sgm, mk.reshape(bp, n_mem, mem_w), mv.reshape(bp, n_mem, mem_w), ROW_TILE, mem_dh)
        zd = _attn_prompt(q.reshape(bp, sp, diff_w), kb.reshape(bp, sp, diff_w), vt,
                          sgd.reshape(bp, sp, diff_w), bias_p, lamp, gcol, lam_init)
        outs["kp"].append(kf.reshape(bp, sp, DIFF_H, HEAD_W))
        outs["vp"].append(vf.reshape(bp, sp, DIFF_H, HEAD_W))
        outs["cp"].append(utail[:, 8 - (CONV_K - 1):, :])
        outs["mkp"].append(mk.reshape(bp, n_mem, MEM_H, mem_dh))
        outs["mvp"].append(mv.reshape(bp, n_mem, MEM_H, mem_dh))
        xp = _out_proj(zd.reshape(bp * sp, diff_w), zc, zm, xp, w_o, g_ln, b_ln, ROW_TILE, alpha)

        q, kb, kf = _proj_qk(xs, w_qk, ROW_TILE)
        vb, vf, sgd = _proj_v(xs, w_v, ROW_TILE, bs, False)
        conv_init = jnp.pad(cache_conv[l].astype(f32), ((0, 0), (0, ss - (CONV_K - 1)), (0, 0)))
        zc, u = _proj_conv(xs, w_cv, cw, conv_init.reshape(bs * ss, conv_ch), ROW_TILE, ss)
        mq, sgm = _proj_mem(xs, w_mq, ROW_TILE, mem_dh)
        zm = _mem_attn(mq, sgm, cache_mem_k[l].reshape(bs, n_mem, mem_w), cache_mem_v[l].reshape(bs, n_mem, mem_w),
                       ss, mem_dh)
        zd = _attn_sample(q.reshape(bs, ss, diff_w), kb.reshape(bs, ss, diff_w), vb.reshape(bs, ss, diff_w),
                          cache_diff_k[l].reshape(bs, past, diff_w), cache_diff_v[l].reshape(bs, past, diff_w),
                          sgd.reshape(bs, ss, diff_w), bias_s, lamp, grow, lam_init)
        outs["ks"].append(kf.reshape(bs, ss, DIFF_H, HEAD_W))
        outs["vs"].append(vf.reshape(bs, ss, DIFF_H, HEAD_W))
        outs["cs"].append(u.reshape(bs, ss, conv_ch)[:, ss - (CONV_K - 1):, :])
        xs = _out_proj(zd.reshape(bs * ss, diff_w), zc, zm, xs, w_o, g_ln, b_ln, ROW_TILE, alpha)

    return (xp.reshape(bp, sp, d_model), xs.reshape(bs, ss, d_model),
            jnp.stack(outs["kp"]), jnp.stack(outs["vp"]), jnp.stack(outs["cp"]), jnp.stack(outs["mkp"]),
            jnp.stack(outs["mvp"]), jnp.stack(outs["ks"]), jnp.stack(outs["vs"]), jnp.stack(outs["cs"]))
```

```python
import functools
import math

import jax
import jax.numpy as jnp
import numpy as np
from jax import lax
from jax.experimental import pallas as pl
from jax.experimental.pallas import tpu as pltpu

f32 = jnp.float32
bf16 = jnp.bfloat16

CHUNK = 64
DIFF_H = 8
DIFF_DH = 64
HEAD_W = 2 * DIFF_DH
MEM_H = 4
CONV_K = 3
N_BUCKETS = 32
MAX_DISTANCE = 128
EPS = 1e-5

LOG2E = 1.4426950408889634
NEG_BIG = -1e30
MASKED_BUCKET = N_BUCKETS

ROW_TILE = 512
ATTN_BLOCK = 512
SUM_ROWS = 16
VMEM_LIMIT = 56 * 1024 * 1024


def _params(n_axes):
    return pltpu.CompilerParams(dimension_semantics=("arbitrary",) * n_axes,
                                vmem_limit_bytes=VMEM_LIMIT)


def _silu(x):
    return x / (1.0 + jnp.exp(-x))


def _t5_bucket(rel):
    half = N_BUCKETS // 2
    max_exact = half // 2
    ret = jnp.where(rel > 0, half, 0)
    n = jnp.abs(rel)
    nf = jnp.maximum(n, 1).astype(f32)
    large = max_exact + (jnp.log(nf / max_exact) / math.log(MAX_DISTANCE / max_exact)
                         * (half - max_exact)).astype(jnp.int32)
    large = jnp.minimum(large, half - 1)
    return ret + jnp.where(n < max_exact, n, large)


def _bias_kernel(tab_ref, far_ref, bkt_ref, out_ref, *, subtract_far):
    h = pl.program_id(0)
    bkt = bkt_ref[...]
    val = jnp.zeros(bkt.shape, f32)
    for b in range(N_BUCKETS):
        val = jnp.where(bkt == b, tab_ref[b, h], val)
    if subtract_far:
        val = val - tab_ref[far_ref[0], h]
    out_ref[0] = jnp.where(bkt == MASKED_BUCKET, NEG_BIG, val * LOG2E)


def _bias_tiles(table, buckets, far_bucket, subtract_far):
    r, c = buckets.shape
    return pl.pallas_call(
        functools.partial(_bias_kernel, subtract_far=subtract_far),
        grid=(DIFF_H,),
        in_specs=[pl.BlockSpec(memory_space=pltpu.SMEM),
                  pl.BlockSpec(memory_space=pltpu.SMEM),
                  pl.BlockSpec((r, c), lambda h: (0, 0))],
        out_specs=pl.BlockSpec((1, r, c), lambda h: (h, 0, 0)),
        out_shape=jax.ShapeDtypeStruct((DIFF_H, r, c), f32),
        compiler_params=_params(1),
        name="bias_tiles",
    )(table, far_bucket, buckets)


def _prompt_buckets(blk):
    kk = jnp.arange(2 * blk)[:, None] - blk
    qq = jnp.arange(blk)[None, :]
    bkt = _t5_bucket(kk - qq)
    visible = (kk // CHUNK) <= (qq // CHUNK)
    return jnp.where(visible, bkt, MASKED_BUCKET).astype(jnp.int32)


def _sample_buckets(past, ss):
    qpos = past + jnp.arange(ss)[:, None]
    kpos = jnp.arange(past + ss)[None, :]
    bkt = _t5_bucket(kpos - qpos)
    visible = (kpos // CHUNK) <= (qpos // CHUNK)
    bkt = jnp.where(visible, bkt, MASKED_BUCKET).astype(jnp.int32)
    return jnp.concatenate([bkt, bkt], axis=0)


def _dot(a, b):
    return jnp.dot(a, b, preferred_element_type=f32)


def _proj_qk_kernel(x_ref, w_ref, q_ref, kb_ref, kf_ref, *, width, q_scale):
    xb = x_ref[...].astype(bf16)
    q_ref[...] = (_dot(xb, w_ref[:, :width]) * q_scale).astype(bf16)
    k = _dot(xb, w_ref[:, width:])
    kf_ref[...] = k
    kb_ref[...] = k.astype(bf16)


def _proj_v_kernel(x_ref, w_ref, vb_ref, vf_ref, sg_ref, *, width, transpose_v):
    xb = x_ref[...].astype(bf16)
    v = _dot(xb, w_ref[:, :width])
    vf_ref[...] = v
    if transpose_v:
        vb_ref[0] = v.T.astype(bf16)
    else:
        vb_ref[...] = v.astype(bf16)
    sg_ref[...] = _silu(_dot(xb, w_ref[:, width:])).astype(bf16)


def _proj_conv_kernel(x_ref, w_ref, cw_ref, init_ref, z_ref, u_ref, carry_ref, *, width, seq_len):
    rows_n = x_ref.shape[0]
    xb = x_ref[...].astype(bf16)
    h = _dot(xb, w_ref[:, 0 * width:1 * width])
    bg = _dot(xb, w_ref[:, 1 * width:2 * width])
    cg = _dot(xb, w_ref[:, 2 * width:3 * width])
    gc = _dot(xb, w_ref[:, 3 * width:4 * width])
    u = cg * h
    rows = lax.broadcasted_iota(jnp.int32, u.shape, 0)
    if seq_len >= rows_n:
        tiles_per_seq = seq_len // rows_n
        @pl.when(pl.program_id(0) % tiles_per_seq == 0)
        def _():
            carry_ref[...] = init_ref[...]

        carry = carry_ref[...]
        prev1 = jnp.where(rows == 0, carry[7:8], pltpu.roll(u, 1, 0))
        prev2 = jnp.where(rows == 0, carry[6:7], jnp.where(rows == 1, carry[7:8], pltpu.roll(u, 2, 0)))
        carry_ref[...] = u[rows_n - 8:]
        u_ref[0] = u[rows_n - 8:]
    else:
        t = rows % seq_len
        init = init_ref[...]
        prev1 = jnp.where(t == 0, pltpu.roll(init, rows_n - 1, 0), pltpu.roll(u, 1, 0))
        prev2 = jnp.where(t < 2, init, pltpu.roll(u, 2, 0))
        u_ref[...] = u
    cw = cw_ref[...]
    y = cw[0:1] * prev2 + cw[1:2] * prev1 + cw[2:3] * u
    z_ref[...] = ((bg * y) * _silu(gc)).astype(bf16)


def _proj_mem_kernel(x_ref, w_ref, mq_ref, sg_ref, *, width, q_scale):
    xb = x_ref[...].astype(bf16)
    mq_ref[...] = (_dot(xb, w_ref[:, :width]) * q_scale).astype(bf16)
    sg_ref[...] = _silu(_dot(xb, w_ref[:, width:])).astype(bf16)


def _row_spec(tm, n):
    return pl.BlockSpec((tm, n), lambda m: (m, 0))


def _full_spec(shape):
    return pl.BlockSpec(shape, lambda m: (0,) * len(shape))


def _proj_qk(x, w, tm):
    rows, d = x.shape
    width = w.shape[1] // 2
    sds = jax.ShapeDtypeStruct
    return pl.pallas_call(
        functools.partial(_proj_qk_kernel, width=width, q_scale=DIFF_DH ** -0.5 * LOG2E),
        grid=(rows // tm,),
        in_specs=[_row_spec(tm, d), _full_spec(w.shape)],
        out_specs=[_row_spec(tm, width)] * 3,
        out_shape=[sds((rows, width), bf16), sds((rows, width), bf16), sds((rows, width), f32)],
        compiler_params=_params(1),
        name="proj_qk",
    )(x, w)


def _proj_v(x, w, tm, n_seq, transpose_v):
    rows, d = x.shape
    width = w.shape[1] // 2
    seq = rows // n_seq
    sds = jax.ShapeDtypeStruct
    if transpose_v:
        tiles = seq // tm
        vb_spec = pl.BlockSpec((1, width, tm), lambda m: (m // tiles, 0, m % tiles))
        vb_shape = sds((n_seq, width, seq), bf16)
    else:
        vb_spec = _row_spec(tm, width)
        vb_shape = sds((rows, width), bf16)
    return pl.pallas_call(
        functools.partial(_proj_v_kernel, width=width, transpose_v=transpose_v),
        grid=(rows // tm,),
        in_specs=[_row_spec(tm, d), _full_spec(w.shape)],
        out_specs=[vb_spec, _row_spec(tm, width), _row_spec(tm, width)],
        out_shape=[vb_shape, sds((rows, width), f32), sds((rows, width), bf16)],
        compiler_params=_params(1),
        name="proj_v",
    )(x, w)


def _proj_conv(x, w, conv_w, init, tm, seq_len):
    rows, d = x.shape
    width = w.shape[1] // 4
    sds = jax.ShapeDtypeStruct
    if seq_len >= tm:
        tiles = seq_len // tm
        u_spec = pl.BlockSpec((1, 8, width), lambda m: (m // tiles, 0, 0))
        u_shape = sds((rows // seq_len, 8, width), f32)
        init_spec = _full_spec(init.shape)
    else:
        u_spec = _row_spec(tm, width)
        u_shape = sds((rows, width), f32)
        init_spec = _row_spec(tm, width)
    return pl.pallas_call(
        functools.partial(_proj_conv_kernel, width=width, seq_len=seq_len),
        grid=(rows // tm,),
        in_specs=[_row_spec(tm, d), _full_spec(w.shape), _full_spec(conv_w.shape), init_spec],
        out_specs=[_row_spec(tm, width), u_spec],
        out_shape=[sds((rows, width), bf16), u_shape],
        scratch_shapes=[pltpu.VMEM((8, width), f32)],
        compiler_params=_params(1),
        name="proj_conv",
    )(x, w, conv_w, init)


def _proj_mem(x, w, tm, head_dim):
    rows, d = x.shape
    width = w.shape[1] // 2
    sds = jax.ShapeDtypeStruct
    return pl.pallas_call(
        functools.partial(_proj_mem_kernel, width=width, q_scale=head_dim ** -0.5 * LOG2E),
        grid=(rows // tm,),
        in_specs=[_row_spec(tm, d), _full_spec(w.shape)],
        out_specs=[_row_spec(tm, width)] * 2,
        out_shape=[sds((rows, width), bf16)] * 2,
        compiler_params=_params(1),
        name="proj_mem",
    )(x, w)


def _memkv_kernel(x_ref, w_ref, k_ref, v_ref, *, width):
    xb = x_ref[...].astype(bf16)
    k_ref[...] = _dot(xb, w_ref[:, :width])
    v_ref[...] = _dot(xb, w_ref[:, width:])


def _memkv(x, w):
    rows, d = x.shape
    width = w.shape[1] // 2
    sds = jax.ShapeDtypeStruct
    return pl.pallas_call(
        functools.partial(_memkv_kernel, width=width),
        grid=(1,),
        in_specs=[_full_spec(x.shape), _full_spec(w.shape)],
        out_specs=[_full_spec((rows, width))] * 2,
        out_shape=[sds((rows, width), f32)] * 2,
        compiler_params=_params(1),
        name="mem_kv",
    )(x, w)


def _mem_attn_kernel(q_ref, sg_ref, k_ref, v_ref, z_ref, *, head_dim):
    for h in range(q_ref.shape[1] // head_dim):
        cols = slice(h * head_dim, (h + 1) * head_dim)
        kh = k_ref[0, :, cols].astype(bf16)
        vh = v_ref[0, :, cols].astype(bf16)
        s = lax.dot_general(q_ref[:, cols], kh, (((1,), (1,)), ((), ())), preferred_element_type=f32)
        p = jnp.exp2(s - jnp.max(s, axis=-1, keepdims=True))
        o = _dot(p.astype(bf16), vh) / jnp.sum(p, axis=-1, keepdims=True)
        z_ref[:, cols] = (o * sg_ref[:, cols].astype(f32)).astype(bf16)


def _mem_attn(q, sg, mk, mv, tm, head_dim, n_seq, seq_base=0):
    rows, width = q.shape
    n_mem = mk.shape[1]
    tiles = (rows // n_seq) // tm
    kv_spec = pl.BlockSpec((1, n_mem, width), lambda m: (seq_base + m // tiles, 0, 0))
    return pl.pallas_call(
        functools.partial(_mem_attn_kernel, head_dim=head_dim),
        grid=(rows // tm,),
        in_specs=[_row_spec(tm, width), _row_spec(tm, width), kv_spec, kv_spec],
        out_specs=_row_spec(tm, width),
        out_shape=jax.ShapeDtypeStruct((rows, width), bf16),
        compiler_params=_params(1),
        name="mem_attn",
    )(q, sg, mk, mv)


def _lambda(lamp_ref, lam_init):
    lp = lamp_ref[...]
    s1 = jnp.sum(lp[0:1] * lp[1:2], axis=1, keepdims=True)
    s2 = jnp.sum(lp[2:3] * lp[3:4], axis=1, keepdims=True)
    return jnp.exp(s1) - jnp.exp(s2) + lam_init


def _two_maps(q):
    lane = lax.broadcasted_iota(jnp.int32, q.shape, 1)
    zero = jnp.zeros_like(q)
    return jnp.concatenate([jnp.where(lane < DIFF_DH, q, zero), jnp.where(lane >= DIFF_DH, q, zero)], axis=0)


def _attn_prompt_kernel(q_ref, k_ref, vt_ref, sg_ref, bias_ref, lamp_ref, gcol_ref, o_ref,
                        m_ref, mx_ref, alpha_ref, acc_ref, s_ref, p_ref, *, blk, lam_init):
    i = pl.program_id(2)
    qcat = _two_maps(q_ref[0])
    ones = jnp.ones((SUM_ROWS, blk), bf16)
    m_ref[...] = jnp.full(m_ref.shape, NEG_BIG, f32)
    acc_ref[...] = jnp.zeros(acc_ref.shape, f32)
    p_ref[...] = jnp.zeros(p_ref.shape, bf16)
    alpha_ref[...] = jnp.ones(alpha_ref.shape, f32)

    def put_scores(j, bias):
        kb = k_ref[0, pl.ds(pl.multiple_of(j * blk, blk), blk), :]
        st = lax.dot_general(kb, qcat, (((1,), (1,)), ((), ())), preferred_element_type=f32)
        if bias is not None:
            st = st + jnp.concatenate([bias, bias], axis=1)
        s_ref[...] = st
        mx_ref[...] = jnp.max(st, axis=0, keepdims=True)

    def weighted_values(j):
        vtb = jnp.concatenate([vt_ref[0, :, pl.ds(pl.multiple_of(j * blk, blk), blk)], ones], axis=0)
        return _dot(vtb, p_ref[...])[:acc_ref.shape[0]]

    def step(j, next_bias, prefetch):
        pv_prev = weighted_values(jnp.maximum(j - 1, 0))
        m_old = m_ref[...]
        m_new = jnp.maximum(m_old, mx_ref[...])
        p = jnp.exp2(s_ref[...] - m_new).astype(bf16)
        if prefetch:
            put_scores(j + 1, next_bias)
        acc_ref[...] = alpha_ref[...] * acc_ref[...] + pv_prev
        alpha_ref[...] = jnp.exp2(m_old - m_new)
        m_ref[...] = m_new
        p_ref[...] = p

    bias_prev = lambda: bias_ref[0, :blk, :]
    bias_diag = lambda: bias_ref[0, blk:, :]

    @pl.when(i == 0)
    def _():
        put_scores(0, bias_diag())

    @pl.when(i == 1)
    def _():
        put_scores(0, bias_prev())

    @pl.when(i >= 2)
    def _():
        put_scores(0, None)

    n_far = jnp.maximum(i - 2, 0)

    def far_pair(g, carry):
        step(2 * g, None, True)
        step(2 * g + 1, None, True)
        return carry

    lax.fori_loop(0, n_far // 2, far_pair, 0)

    @pl.when(n_far % 2 == 1)
    def _():
        step(n_far - 1, None, True)

    @pl.when(i >= 2)
    def _():
        step(i - 2, bias_prev(), True)

    @pl.when(i >= 1)
    def _():
        step(i - 1, bias_diag(), True)

    step(i, None, False)
    acc_ref[...] = alpha_ref[...] * acc_ref[...] + weighted_values(i)

    lam = _lambda(lamp_ref, lam_init)
    ot = acc_ref[:HEAD_W, :] * (1.0 / acc_ref[HEAD_W:HEAD_W + 1, :])
    d = ot[:, :blk] - lam * ot[:, blk:]
    ms = jnp.mean(d * d, axis=0, keepdims=True)
    y = d * lax.rsqrt(ms + EPS) * gcol_ref[...]
    o_ref[0] = (y.T * sg_ref[0].astype(f32)).astype(bf16)


def _attn_prompt(q, k, vt, sg, bias, lamp, gcol, lam_init):
    b, s, w = q.shape
    blk = ATTN_BLOCK
    qspec = pl.BlockSpec((1, blk, HEAD_W), lambda bi, h, i: (bi, i, h))
    return pl.pallas_call(
        functools.partial(_attn_prompt_kernel, blk=blk, lam_init=lam_init),
        grid=(b, DIFF_H, s // blk),
        in_specs=[qspec,
                  pl.BlockSpec((1, s, HEAD_W), lambda bi, h, i: (bi, 0, h)),
                  pl.BlockSpec((1, HEAD_W, s), lambda bi, h, i: (bi, h, 0)),
                  qspec,
                  pl.BlockSpec((1, 2 * blk, blk), lambda bi, h, i: (h, 0, 0)),
                  pl.BlockSpec(lamp.shape, lambda bi, h, i: (0, 0)),
                  pl.BlockSpec(gcol.shape, lambda bi, h, i: (0, 0))],
        out_specs=qspec,
        out_shape=jax.ShapeDtypeStruct((b, s, w), bf16),
        scratch_shapes=[pltpu.VMEM((1, 2 * blk), f32)] * 3 + [
                        pltpu.VMEM((HEAD_W + 8, 2 * blk), f32),
                        pltpu.VMEM((blk, 2 * blk), f32), pltpu.VMEM((blk, 2 * blk), bf16)],
        compiler_params=_params(3),
        name="attn_prompt",
    )(q, k, vt, sg, bias, lamp, gcol)


def _attn_sample_kernel(q_ref, kn_ref, vn_ref, kc_ref, vc_ref, sg_ref, bias_ref, lamp_ref, grow_ref, o_ref,
                        *, past, lam_init):
    ss = q_ref.shape[1]
    lam = _lambda(lamp_ref, lam_init)
    nt = (((1,), (1,)), ((), ()))
    for h in range(DIFF_H):
        cols = slice(h * HEAD_W, (h + 1) * HEAD_W)
        qcat = _two_maps(q_ref[0, :, cols])
        sc = lax.dot_general(qcat, kc_ref[0, :, cols].astype(bf16), nt, preferred_element_type=f32)
        sn = lax.dot_general(qcat, kn_ref[0, :, cols], nt, preferred_element_type=f32)
        sc = sc + bias_ref[h, :, :past]
        sn = sn + bias_ref[h, :, past:]
        m = jnp.maximum(jnp.max(sc, axis=-1, keepdims=True), jnp.max(sn, axis=-1, keepdims=True))
        pc = jnp.exp2(sc - m)
        pn = jnp.exp2(sn - m)
        l = jnp.sum(pc, axis=-1, keepdims=True) + jnp.sum(pn, axis=-1, keepdims=True)
        o = (_dot(pc.astype(bf16), vc_ref[0, :, cols].astype(bf16)) + _dot(pn.astype(bf16), vn_ref[0, :, cols])) / l
        d = o[:ss] - lam * o[ss:]
        ms = jnp.mean(d * d, axis=-1, keepdims=True)
        y = d * lax.rsqrt(ms + EPS) * grow_ref[...]
        o_ref[0, :, cols] = (y * sg_ref[0, :, cols].astype(f32)).astype(bf16)


def _attn_sample(q, kn, vn, kc, vc, sg, bias, lamp, grow, lam_init, cache_base):
    b, ss, w = q.shape
    past = kc.shape[1]
    new_spec = pl.BlockSpec((1, ss, w), lambda bi: (bi, 0, 0))
    cache_spec = pl.BlockSpec((1, past, w), lambda bi: (cache_base + bi, 0, 0))
    return pl.pallas_call(
        functools.partial(_attn_sample_kernel, past=past, lam_init=lam_init),
        grid=(b,),
        in_specs=[new_spec, new_spec, new_spec, cache_spec, cache_spec, new_spec,
                  _full_spec(bias.shape), _full_spec(lamp.shape), _full_spec(grow.shape)],
        out_specs=new_spec,
        out_shape=jax.ShapeDtypeStruct((b, ss, w), bf16),
        compiler_params=_params(1),
        name="attn_sample",
    )(q, kn, vn, kc, vc, sg, bias, lamp, grow)


def _out_kernel(zd_ref, zc_ref, zm_ref, x_ref, w_ref, g_ref, b_ref, y_ref, *, alpha):
    wd = zd_ref.shape[1]
    wc = zc_ref.shape[1]
    o = (_dot(zd_ref[...], w_ref[:wd]) + _dot(zc_ref[...], w_ref[wd:wd + wc])
         + _dot(zm_ref[...], w_ref[wd + wc:]))
    r = alpha * x_ref[...] + o
    mu = jnp.mean(r, axis=-1, keepdims=True)
    c = r - mu
    var = jnp.mean(c * c, axis=-1, keepdims=True)
    y_ref[...] = c * lax.rsqrt(var + EPS) * g_ref[...] + b_ref[...]


def _out_proj(zd, zc, zm, x, w, g, b, tm, alpha):
    rows, d = x.shape
    return pl.pallas_call(
        functools.partial(_out_kernel, alpha=alpha),
        grid=(rows // tm,),
        in_specs=[_row_spec(tm, zd.shape[1]), _row_spec(tm, zc.shape[1]), _row_spec(tm, zm.shape[1]),
                  _row_spec(tm, d), _full_spec(w.shape), _full_spec(g.shape), _full_spec(b.shape)],
        out_specs=_row_spec(tm, d),
        out_shape=jax.ShapeDtypeStruct((rows, d), f32),
        compiler_params=_params(1),
        name="out_proj",
    )(zd, zc, zm, x, w, g, b)


def kernel(x_prompt, x_sample, cache_diff_k, cache_diff_v, cache_conv, cache_mem_k, cache_mem_v, mem_prompt,
           rel_bias_table, w_in, w_mem_kv, conv_w, lambda_q1, lambda_k1, lambda_q2, lambda_k2, subln_g, w_out,
           ln_g, ln_b):
    depth = w_in.shape[0]
    bp, sp, d_model = x_prompt.shape
    bs, ss, _ = x_sample.shape
    past = cache_diff_k.shape[2]
    n_mem = mem_prompt.shape[1]
    diff_w = DIFF_H * HEAD_W
    conv_ch = conv_w.shape[2]
    mem_w = cache_mem_k.shape[3] * cache_mem_k.shape[4]
    mem_dh = cache_mem_k.shape[4]
    alpha = (2 * depth) ** 0.25
    blk = ATTN_BLOCK
    assert sp % blk == 0 and blk % CHUNK == 0 and sp % ROW_TILE == 0 and (bs * ss) % ROW_TILE == 0
    assert conv_w.shape[1] == CONV_K and ss >= CONV_K - 1

    widths = (diff_w, diff_w, diff_w, conv_ch, conv_ch, conv_ch, mem_w, diff_w, conv_ch, mem_w)
    offs = np.concatenate([[0], np.cumsum(widths)])
    c_q, c_k, c_v, c_h, c_bg, c_cg, c_mq, c_gd, c_gc, c_gm = [slice(int(a), int(b)) for a, b in zip(offs[:-1], offs[1:])]

    far_bucket = _t5_bucket(jnp.array([-(blk + 1)], jnp.int32)).astype(jnp.int32)
    bias_p = _bias_tiles(rel_bias_table, _prompt_buckets(blk), far_bucket, True)
    bias_s = _bias_tiles(rel_bias_table, _sample_buckets(past, ss), far_bucket, False)

    xp = x_prompt.reshape(bp * sp, d_model)
    xs = x_sample.reshape(bs * ss, d_model)
    outs = {k: [] for k in ("kp", "vp", "cp", "mkp", "mvp", "ks", "vs", "cs")}

    for l in range(depth):
        lam_init = 0.8 - 0.6 * math.exp(-0.3 * l)
        wl = w_in[l]
        w_qk = jnp.concatenate([wl[:, c_q], wl[:, c_k]], axis=1).astype(bf16)
        w_v = jnp.concatenate([wl[:, c_v], wl[:, c_gd]], axis=1).astype(bf16)
        w_cv = jnp.concatenate([wl[:, c_h], wl[:, c_bg], wl[:, c_cg], wl[:, c_gc]], axis=1).astype(bf16)
        w_mq = jnp.concatenate([wl[:, c_mq], wl[:, c_gm]], axis=1).astype(bf16)
        w_o = w_out[l].astype(bf16)
        w_kv = w_mem_kv[l].astype(bf16)
        lamp = jnp.stack([lambda_q1[l], lambda_k1[l], lambda_q2[l], lambda_k2[l]]).astype(f32)
        gscaled = subln_g[l].astype(f32) * (1.0 - lam_init)
        gcol = jnp.broadcast_to(gscaled[:, None], (HEAD_W, blk))
        grow = gscaled[None, :]
        g_ln = ln_g[l][None, :].astype(f32)
        b_ln = ln_b[l][None, :].astype(f32)
        cw = conv_w[l].astype(f32)

        q, kb, kf = _proj_qk(xp, w_qk, ROW_TILE)
        vt, vf, sgd = _proj_v(xp, w_v, ROW_TILE, bp, True)
        zc, utail = _proj_conv(xp, w_cv, cw, jnp.zeros((8, conv_ch), f32), ROW_TILE, sp)
        mq, sgm = _proj_mem(xp, w_mq, ROW_TILE, mem_dh)
        mk, mv = _memkv(mem_prompt.reshape(bp * n_mem, d_model), w_kv)
        zm = _mem_attn(mq, sgm, mk.reshape(bp, n_mem, mem_w), mv.reshape(bp, n_mem, mem_w), ROW_TILE, mem_dh, bp)
        zd = _attn_prompt(q.reshape(bp, sp, diff_w), kb.reshape(bp, sp, diff_w), vt,
                          sgd.reshape(bp, sp, diff_w), bias_p, lamp, gcol, lam_init)
        outs["kp"].append(kf.reshape(bp, sp, DIFF_H, HEAD_W))
        outs["vp"].append(vf.reshape(bp, sp, DIFF_H, HEAD_W))
        outs["cp"].append(utail[:, 8 - (CONV_K - 1):, :])
        outs["mkp"].append(mk.reshape(bp, n_mem, MEM_H, mem_dh))
        outs["mvp"].append(mv.reshape(bp, n_mem, MEM_H, mem_dh))
        xp = _out_proj(zd.reshape(bp * sp, diff_w), zc, zm, xp, w_o, g_ln, b_ln, ROW_TILE, alpha)

        q, kb, kf = _proj_qk(xs, w_qk, ROW_TILE)
        vb, vf, sgd = _proj_v(xs, w_v, ROW_TILE, bs, False)
        conv_init = jnp.pad(cache_conv[l].astype(f32), ((0, 0), (0, ss - (CONV_K - 1)), (0, 0)))
        zc, u = _proj_conv(xs, w_cv, cw, conv_init.reshape(bs * ss, conv_ch), ROW_TILE, ss)
        mq, sgm = _proj_mem(xs, w_mq, ROW_TILE, mem_dh)
        zm = _mem_attn(mq, sgm, cache_mem_k.reshape(depth * bs, n_mem, mem_w),
                       cache_mem_v.reshape(depth * bs, n_mem, mem_w), ss, mem_dh, bs, l * bs)
        zd = _attn_sample(q.reshape(bs, ss, diff_w), kb.reshape(bs, ss, diff_w), vb.reshape(bs, ss, diff_w),
                          cache_diff_k.reshape(depth * bs, past, diff_w),
                          cache_diff_v.reshape(depth * bs, past, diff_w),
                          sgd.reshape(bs, ss, diff_w), bias_s, lamp, grow, lam_init, l * bs)
        outs["ks"].append(kf.reshape(bs, ss, DIFF_H, HEAD_W))
        outs["vs"].append(vf.reshape(bs, ss, DIFF_H, HEAD_W))
        outs["cs"].append(u.reshape(bs, ss, conv_ch)[:, ss - (CONV_K - 1):, :])
        xs = _out_proj(zd.reshape(bs * ss, diff_w), zc, zm, xs, w_o, g_ln, b_ln, ROW_TILE, alpha)

    return (xp.reshape(bp, sp, d_model), xs.reshape(bs, ss, d_model),
            jnp.stack(outs["kp"]), jnp.stack(outs["vp"]), jnp.stack(outs["cp"]), jnp.stack(outs["mkp"]),
            jnp.stack(outs["mvp"]), jnp.stack(outs["ks"]), jnp.stack(outs["vs"]), jnp.stack(outs["cs"]))
```

```python
import functools
import math

import jax
import jax.numpy as jnp
import numpy as np
from jax import lax
from jax.experimental import pallas as pl
from jax.experimental.pallas import tpu as pltpu

f32 = jnp.float32
bf16 = jnp.bfloat16

CHUNK = 64
DIFF_H = 8
DIFF_DH = 64
HEAD_W = 2 * DIFF_DH
MEM_H = 4
CONV_K = 3
N_BUCKETS = 32
MAX_DISTANCE = 128
EPS = 1e-5

LOG2E = 1.4426950408889634
NEG_BIG = -1e30
MASKED_BUCKET = N_BUCKETS

ROW_TILE = 512
ATTN_BLOCK = 512
SUM_ROWS = 16
VMEM_LIMIT = 56 * 1024 * 1024


def _params(n_axes):
    return pltpu.CompilerParams(dimension_semantics=("arbitrary",) * n_axes,
                                vmem_limit_bytes=VMEM_LIMIT)


def _silu(x):
    return x / (1.0 + jnp.exp(-x))


def _t5_bucket(rel):
    half = N_BUCKETS // 2
    max_exact = half // 2
    ret = jnp.where(rel > 0, half, 0)
    n = jnp.abs(rel)
    nf = jnp.maximum(n, 1).astype(f32)
    large = max_exact + (jnp.log(nf / max_exact) / math.log(MAX_DISTANCE / max_exact)
                         * (half - max_exact)).astype(jnp.int32)
    large = jnp.minimum(large, half - 1)
    return ret + jnp.where(n < max_exact, n, large)


def _bias_kernel(tab_ref, far_ref, bkt_ref, out_ref, *, subtract_far):
    h = pl.program_id(0)
    bkt = bkt_ref[...]
    val = jnp.zeros(bkt.shape, f32)
    for b in range(N_BUCKETS):
        val = jnp.where(bkt == b, tab_ref[b, h], val)
    if subtract_far:
        val = val - tab_ref[far_ref[0], h]
    out_ref[0] = jnp.where(bkt == MASKED_BUCKET, NEG_BIG, val * LOG2E)


def _bias_tiles(table, buckets, far_bucket, subtract_far):
    r, c = buckets.shape
    return pl.pallas_call(
        functools.partial(_bias_kernel, subtract_far=subtract_far),
        grid=(DIFF_H,),
        in_specs=[pl.BlockSpec(memory_space=pltpu.SMEM),
                  pl.BlockSpec(memory_space=pltpu.SMEM),
                  pl.BlockSpec((r, c), lambda h: (0, 0))],
        out_specs=pl.BlockSpec((1, r, c), lambda h: (h, 0, 0)),
        out_shape=jax.ShapeDtypeStruct((DIFF_H, r, c), f32),
        compiler_params=_params(1),
        name="bias_tiles",
    )(table, far_bucket, buckets)


def _prompt_buckets(blk):
    kk = jnp.arange(2 * blk)[:, None] - blk
    qq = jnp.arange(blk)[None, :]
    bkt = _t5_bucket(kk - qq)
    visible = (kk // CHUNK) <= (qq // CHUNK)
    return jnp.where(visible, bkt, MASKED_BUCKET).astype(jnp.int32)


def _sample_buckets(past, ss):
    qpos = past + jnp.arange(ss)[:, None]
    kpos = jnp.arange(past + ss)[None, :]
    bkt = _t5_bucket(kpos - qpos)
    visible = (kpos // CHUNK) <= (qpos // CHUNK)
    bkt = jnp.where(visible, bkt, MASKED_BUCKET).astype(jnp.int32)
    return jnp.concatenate([bkt, bkt], axis=0)


def _dot(a, b):
    return jnp.dot(a, b, preferred_element_type=f32)


def _proj_qk_kernel(x_ref, w_ref, q_ref, kb_ref, kf_ref, *, width, q_scale):
    xb = x_ref[...].astype(bf16)
    q_ref[...] = (_dot(xb, w_ref[:, :width]) * q_scale).astype(bf16)
    k = _dot(xb, w_ref[:, width:])
    kf_ref[...] = k
    kb_ref[...] = k.astype(bf16)


def _proj_v_kernel(x_ref, w_ref, vb_ref, vf_ref, sg_ref, *, width, transpose_v):
    xb = x_ref[...].astype(bf16)
    v = _dot(xb, w_ref[:, :width])
    vf_ref[...] = v
    if transpose_v:
        vb_ref[0] = v.T.astype(bf16)
    else:
        vb_ref[...] = v.astype(bf16)
    sg_ref[...] = _silu(_dot(xb, w_ref[:, width:])).astype(bf16)


def _proj_conv_kernel(x_ref, w_ref, cw_ref, init_ref, z_ref, u_ref, carry_ref, *, width, seq_len):
    rows_n = x_ref.shape[0]
    xb = x_ref[...].astype(bf16)
    h = _dot(xb, w_ref[:, 0 * width:1 * width])
    bg = _dot(xb, w_ref[:, 1 * width:2 * width])
    cg = _dot(xb, w_ref[:, 2 * width:3 * width])
    gc = _dot(xb, w_ref[:, 3 * width:4 * width])
    u = cg * h
    rows = lax.broadcasted_iota(jnp.int32, u.shape, 0)
    if seq_len >= rows_n:
        tiles_per_seq = seq_len // rows_n
        @pl.when(pl.program_id(0) % tiles_per_seq == 0)
        def _():
            carry_ref[...] = init_ref[...]

        carry = carry_ref[...]
        prev1 = jnp.where(rows == 0, carry[7:8], pltpu.roll(u, 1, 0))
        prev2 = jnp.where(rows == 0, carry[6:7], jnp.where(rows == 1, carry[7:8], pltpu.roll(u, 2, 0)))
        carry_ref[...] = u[rows_n - 8:]
        u_ref[0] = u[rows_n - 8:]
    else:
        t = rows % seq_len
        init = init_ref[...]
        prev1 = jnp.where(t == 0, pltpu.roll(init, rows_n - 1, 0), pltpu.roll(u, 1, 0))
        prev2 = jnp.where(t < 2, init, pltpu.roll(u, 2, 0))
        u_ref[...] = u
    cw = cw_ref[...]
    y = cw[0:1] * prev2 + cw[1:2] * prev1 + cw[2:3] * u
    z_ref[...] = ((bg * y) * _silu(gc)).astype(bf16)


def _proj_mem_kernel(x_ref, w_ref, mq_ref, sg_ref, *, width, q_scale):
    xb = x_ref[...].astype(bf16)
    mq_ref[...] = (_dot(xb, w_ref[:, :width]) * q_scale).astype(bf16)
    sg_ref[...] = _silu(_dot(xb, w_ref[:, width:])).astype(bf16)


def _row_spec(tm, n):
    return pl.BlockSpec((tm, n), lambda m: (m, 0))


def _full_spec(shape):
    return pl.BlockSpec(shape, lambda m: (0,) * len(shape))


def _proj_qk(x, w, tm):
    rows, d = x.shape
    width = w.shape[1] // 2
    sds = jax.ShapeDtypeStruct
    return pl.pallas_call(
        functools.partial(_proj_qk_kernel, width=width, q_scale=DIFF_DH ** -0.5 * LOG2E),
        grid=(rows // tm,),
        in_specs=[_row_spec(tm, d), _full_spec(w.shape)],
        out_specs=[_row_spec(tm, width)] * 3,
        out_shape=[sds((rows, width), bf16), sds((rows, width), bf16), sds((rows, width), f32)],
        compiler_params=_params(1),
        name="proj_qk",
    )(x, w)


def _proj_v(x, w, tm, n_seq, transpose_v):
    rows, d = x.shape
    width = w.shape[1] // 2
    seq = rows // n_seq
    sds = jax.ShapeDtypeStruct
    if transpose_v:
        tiles = seq // tm
        vb_spec = pl.BlockSpec((1, width, tm), lambda m: (m // tiles, 0, m % tiles))
        vb_shape = sds((n_seq, width, seq), bf16)
    else:
        vb_spec = _row_spec(tm, width)
        vb_shape = sds((rows, width), bf16)
    return pl.pallas_call(
        functools.partial(_proj_v_kernel, width=width, transpose_v=transpose_v),
        grid=(rows // tm,),
        in_specs=[_row_spec(tm, d), _full_spec(w.shape)],
        out_specs=[vb_spec, _row_spec(tm, width), _row_spec(tm, width)],
        out_shape=[vb_shape, sds((rows, width), f32), sds((rows, width), bf16)],
        compiler_params=_params(1),
        name="proj_v",
    )(x, w)


def _proj_conv(x, w, conv_w, init, tm, seq_len):
    rows, d = x.shape
    width = w.shape[1] // 4
    sds = jax.ShapeDtypeStruct
    if seq_len >= tm:
        tiles = seq_len // tm
        u_spec = pl.BlockSpec((1, 8, width), lambda m: (m // tiles, 0, 0))
        u_shape = sds((rows // seq_len, 8, width), f32)
        init_spec = _full_spec(init.shape)
    else:
        u_spec = _row_spec(tm, width)
        u_shape = sds((rows, width), f32)
        init_spec = _row_spec(tm, width)
    return pl.pallas_call(
        functools.partial(_proj_conv_kernel, width=width, seq_len=seq_len),
        grid=(rows // tm,),
        in_specs=[_row_spec(tm, d), _full_spec(w.shape), _full_spec(conv_w.shape), init_spec],
        out_specs=[_row_spec(tm, width), u_spec],
        out_shape=[sds((rows, width), bf16), u_shape],
        scratch_shapes=[pltpu.VMEM((8, width), f32)],
        compiler_params=_params(1),
        name="proj_conv",
    )(x, w, conv_w, init)


def _proj_mem(x, w, tm, head_dim):
    rows, d = x.shape
    width = w.shape[1] // 2
    sds = jax.ShapeDtypeStruct
    return pl.pallas_call(
        functools.partial(_proj_mem_kernel, width=width, q_scale=head_dim ** -0.5 * LOG2E),
        grid=(rows // tm,),
        in_specs=[_row_spec(tm, d), _full_spec(w.shape)],
        out_specs=[_row_spec(tm, width)] * 2,
        out_shape=[sds((rows, width), bf16)] * 2,
        compiler_params=_params(1),
        name="proj_mem",
    )(x, w)


def _memkv_kernel(x_ref, w_ref, k_ref, v_ref, *, width):
    xb = x_ref[...].astype(bf16)
    k_ref[...] = _dot(xb, w_ref[:, :width])
    v_ref[...] = _dot(xb, w_ref[:, width:])


def _memkv(x, w):
    rows, d = x.shape
    width = w.shape[1] // 2
    sds = jax.ShapeDtypeStruct
    return pl.pallas_call(
        functools.partial(_memkv_kernel, width=width),
        grid=(1,),
        in_specs=[_full_spec(x.shape), _full_spec(w.shape)],
        out_specs=[_full_spec((rows, width))] * 2,
        out_shape=[sds((rows, width), f32)] * 2,
        compiler_params=_params(1),
        name="mem_kv",
    )(x, w)


def _mem_attn_kernel(q_ref, sg_ref, k_ref, v_ref, z_ref, *, head_dim):
    for h in range(q_ref.shape[1] // head_dim):
        cols = slice(h * head_dim, (h + 1) * head_dim)
        kh = k_ref[0, :, cols].astype(bf16)
        vh = v_ref[0, :, cols].astype(bf16)
        s = lax.dot_general(q_ref[:, cols], kh, (((1,), (1,)), ((), ())), preferred_element_type=f32)
        p = jnp.exp2(s - jnp.max(s, axis=-1, keepdims=True))
        o = _dot(p.astype(bf16), vh) / jnp.sum(p, axis=-1, keepdims=True)
        z_ref[:, cols] = (o * sg_ref[:, cols].astype(f32)).astype(bf16)


def _mem_attn(q, sg, mk, mv, tm, head_dim, n_seq, seq_base=0):
    rows, width = q.shape
    n_mem = mk.shape[1]
    tiles = (rows // n_seq) // tm
    kv_spec = pl.BlockSpec((1, n_mem, width), lambda m: (seq_base + m // tiles, 0, 0))
    return pl.pallas_call(
        functools.partial(_mem_attn_kernel, head_dim=head_dim),
        grid=(rows // tm,),
        in_specs=[_row_spec(tm, width), _row_spec(tm, width), kv_spec, kv_spec],
        out_specs=_row_spec(tm, width),
        out_shape=jax.ShapeDtypeStruct((rows, width), bf16),
        compiler_params=_params(1),
        name="mem_attn",
    )(q, sg, mk, mv)


def _lambda(lamp_ref, lam_init):
    lp = lamp_ref[...]
    s1 = jnp.sum(lp[0:1] * lp[1:2], axis=1, keepdims=True)
    s2 = jnp.sum(lp[2:3] * lp[3:4], axis=1, keepdims=True)
    return jnp.exp(s1) - jnp.exp(s2) + lam_init


def _two_maps(q):
    lane = lax.broadcasted_iota(jnp.int32, q.shape, 1)
    zero = jnp.zeros_like(q)
    return jnp.concatenate([jnp.where(lane < DIFF_DH, q, zero), jnp.where(lane >= DIFF_DH, q, zero)], axis=0)


def _attn_prompt_kernel(q_ref, k_ref, vt_ref, sg_ref, bias_ref, lamp_ref, gcol_ref, o_ref,
                        m_ref, mx_ref, alpha_ref, acc_ref, s_ref, p_ref, *, blk, lam_init):
    i = pl.program_id(2)
    qcat = _two_maps(q_ref[0])
    ones = jnp.ones((SUM_ROWS, blk), bf16)
    m_ref[...] = jnp.full(m_ref.shape, NEG_BIG, f32)
    acc_ref[...] = jnp.zeros(acc_ref.shape, f32)
    p_ref[...] = jnp.zeros(p_ref.shape, bf16)
    alpha_ref[...] = jnp.ones(alpha_ref.shape, f32)

    def put_scores(j, bias):
        kb = k_ref[0, pl.ds(pl.multiple_of(j * blk, blk), blk), :]
        st = lax.dot_general(kb, qcat, (((1,), (1,)), ((), ())), preferred_element_type=f32)
        if bias is not None:
            st = st + jnp.concatenate([bias, bias], axis=1)
        s_ref[...] = st
        mx_ref[...] = jnp.max(st, axis=0, keepdims=True)

    def weighted_values(j):
        vtb = jnp.concatenate([vt_ref[0, :, pl.ds(pl.multiple_of(j * blk, blk), blk)], ones], axis=0)
        return _dot(vtb, p_ref[...])[:acc_ref.shape[0]]

    def step(j, next_bias, prefetch):
        pv_prev = weighted_values(jnp.maximum(j - 1, 0))
        m_old = m_ref[...]
        m_new = jnp.maximum(m_old, mx_ref[...])
        p = jnp.exp2(s_ref[...] - m_new).astype(bf16)
        if prefetch:
            put_scores(j + 1, next_bias)
        acc_ref[...] = alpha_ref[...] * acc_ref[...] + pv_prev
        alpha_ref[...] = jnp.exp2(m_old - m_new)
        m_ref[...] = m_new
        p_ref[...] = p

    bias_prev = lambda: bias_ref[0, :blk, :]
    bias_diag = lambda: bias_ref[0, blk:, :]

    def last_steps(n):
        if n >= 3:
            step(i - 2, bias_prev(), True)
        if n >= 2:
            step(i - 1, bias_diag(), True)
        step(i, None, False)
        acc_ref[...] = alpha_ref[...] * acc_ref[...] + weighted_values(i)

    @pl.when(i == 0)
    def _():
        put_scores(0, bias_diag())
        last_steps(1)

    @pl.when(i == 1)
    def _():
        put_scores(0, bias_prev())
        last_steps(2)

    @pl.when(i >= 2)
    def _():
        put_scores(0, None)
        n_far = i - 2

        def far_pair(g, carry):
            step(2 * g, None, True)
            step(2 * g + 1, None, True)
            return carry

        lax.fori_loop(0, n_far // 2, far_pair, 0)

        @pl.when(n_far % 2 == 1)
        def _():
            step(n_far - 1, None, True)

        last_steps(3)

    lam = _lambda(lamp_ref, lam_init)
    ot = acc_ref[:HEAD_W, :] * (1.0 / acc_ref[HEAD_W:HEAD_W + 1, :])
    d = ot[:, :blk] - lam * ot[:, blk:]
    ms = jnp.mean(d * d, axis=0, keepdims=True)
    y = d * lax.rsqrt(ms + EPS) * gcol_ref[...]
    o_ref[0] = (y.T * sg_ref[0].astype(f32)).astype(bf16)


def _attn_prompt(q, k, vt, sg, bias, lamp, gcol, lam_init):
    b, s, w = q.shape
    blk = ATTN_BLOCK
    qspec = pl.BlockSpec((1, blk, HEAD_W), lambda bi, h, i: (bi, i, h))
    return pl.pallas_call(
        functools.partial(_attn_prompt_kernel, blk=blk, lam_init=lam_init),
        grid=(b, DIFF_H, s // blk),
        in_specs=[qspec,
                  pl.BlockSpec((1, s, HEAD_W), lambda bi, h, i: (bi, 0, h)),
                  pl.BlockSpec((1, HEAD_W, s), lambda bi, h, i: (bi, h, 0)),
                  qspec,
                  pl.BlockSpec((1, 2 * blk, blk), lambda bi, h, i: (h, 0, 0)),
                  pl.BlockSpec(lamp.shape, lambda bi, h, i: (0, 0)),
                  pl.BlockSpec(gcol.shape, lambda bi, h, i: (0, 0))],
        out_specs=qspec,
        out_shape=jax.ShapeDtypeStruct((b, s, w), bf16),
        scratch_shapes=[pltpu.VMEM((1, 2 * blk), f32)] * 3 + [
                        pltpu.VMEM((HEAD_W + 8, 2 * blk), f32),
                        pltpu.VMEM((blk, 2 * blk), f32), pltpu.VMEM((blk, 2 * blk), bf16)],
        compiler_params=_params(3),
        name="attn_prompt",
    )(q, k, vt, sg, bias, lamp, gcol)


def _attn_sample_kernel(q_ref, kn_ref, vn_ref, kc_ref, vc_ref, sg_ref, bias_ref, lamp_ref, grow_ref, o_ref,
                        *, past, lam_init):
    ss = q_ref.shape[1]
    lam = _lambda(lamp_ref, lam_init)
    nt = (((1,), (1,)), ((), ()))
    for h in range(DIFF_H):
        cols = slice(h * HEAD_W, (h + 1) * HEAD_W)
        qcat = _two_maps(q_ref[0, :, cols])
        head_rows = pl.ds(h, past, stride=DIFF_H)
        sc = lax.dot_general(qcat, kc_ref[0, head_rows, :].astype(bf16), nt, preferred_element_type=f32)
        sn = lax.dot_general(qcat, kn_ref[0, :, cols], nt, preferred_element_type=f32)
        sc = sc + bias_ref[h, :, :past]
        sn = sn + bias_ref[h, :, past:]
        m = jnp.maximum(jnp.max(sc, axis=-1, keepdims=True), jnp.max(sn, axis=-1, keepdims=True))
        pc = jnp.exp2(sc - m)
        pn = jnp.exp2(sn - m)
        l = jnp.sum(pc, axis=-1, keepdims=True) + jnp.sum(pn, axis=-1, keepdims=True)
        o = (_dot(pc.astype(bf16), vc_ref[0, head_rows, :].astype(bf16)) + _dot(pn.astype(bf16), vn_ref[0, :, cols])) / l
        d = o[:ss] - lam * o[ss:]
        ms = jnp.mean(d * d, axis=-1, keepdims=True)
        y = d * lax.rsqrt(ms + EPS) * grow_ref[...]
        o_ref[0, :, cols] = (y * sg_ref[0, :, cols].astype(f32)).astype(bf16)


def _attn_sample(q, kn, vn, kc, vc, sg, bias, lamp, grow, lam_init, cache_base):
    b, ss, w = q.shape
    past = kc.shape[1] // DIFF_H
    new_spec = pl.BlockSpec((1, ss, w), lambda bi: (bi, 0, 0))
    cache_spec = pl.BlockSpec((1, past * DIFF_H, HEAD_W), lambda bi: (cache_base + bi, 0, 0))
    return pl.pallas_call(
        functools.partial(_attn_sample_kernel, past=past, lam_init=lam_init),
        grid=(b,),
        in_specs=[new_spec, new_spec, new_spec, cache_spec, cache_spec, new_spec,
                  _full_spec(bias.shape), _full_spec(lamp.shape), _full_spec(grow.shape)],
        out_specs=new_spec,
        out_shape=jax.ShapeDtypeStruct((b, ss, w), bf16),
        compiler_params=_params(1),
        name="attn_sample",
    )(q, kn, vn, kc, vc, sg, bias, lamp, grow)


def _out_kernel(zd_ref, zc_ref, zm_ref, x_ref, w_ref, g_ref, b_ref, y_ref, *, alpha):
    wd = zd_ref.shape[1]
    wc = zc_ref.shape[1]
    o = (_dot(zd_ref[...], w_ref[:wd]) + _dot(zc_ref[...], w_ref[wd:wd + wc])
         + _dot(zm_ref[...], w_ref[wd + wc:]))
    r = alpha * x_ref[...] + o
    mu = jnp.mean(r, axis=-1, keepdims=True)
    c = r - mu
    var = jnp.mean(c * c, axis=-1, keepdims=True)
    y_ref[...] = c * lax.rsqrt(var + EPS) * g_ref[...] + b_ref[...]


def _out_proj(zd, zc, zm, x, w, g, b, tm, alpha):
    rows, d = x.shape
    return pl.pallas_call(
        functools.partial(_out_kernel, alpha=alpha),
        grid=(rows // tm,),
        in_specs=[_row_spec(tm, zd.shape[1]), _row_spec(tm, zc.shape[1]), _row_spec(tm, zm.shape[1]),
                  _row_spec(tm, d), _full_spec(w.shape), _full_spec(g.shape), _full_spec(b.shape)],
        out_specs=_row_spec(tm, d),
        out_shape=jax.ShapeDtypeStruct((rows, d), f32),
        compiler_params=_params(1),
        name="out_proj",
    )(zd, zc, zm, x, w, g, b)


def kernel(x_prompt, x_sample, cache_diff_k, cache_diff_v, cache_conv, cache_mem_k, cache_mem_v, mem_prompt,
           rel_bias_table, w_in, w_mem_kv, conv_w, lambda_q1, lambda_k1, lambda_q2, lambda_k2, subln_g, w_out,
           ln_g, ln_b):
    depth = w_in.shape[0]
    bp, sp, d_model = x_prompt.shape
    bs, ss, _ = x_sample.shape
    past = cache_diff_k.shape[2]
    n_mem = mem_prompt.shape[1]
    diff_w = DIFF_H * HEAD_W
    conv_ch = conv_w.shape[2]
    mem_w = cache_mem_k.shape[3] * cache_mem_k.shape[4]
    mem_dh = cache_mem_k.shape[4]
    alpha = (2 * depth) ** 0.25
    blk = ATTN_BLOCK
    assert sp % blk == 0 and blk % CHUNK == 0 and sp % ROW_TILE == 0 and (bs * ss) % ROW_TILE == 0
    assert conv_w.shape[1] == CONV_K and ss >= CONV_K - 1

    widths = (diff_w, diff_w, diff_w, conv_ch, conv_ch, conv_ch, mem_w, diff_w, conv_ch, mem_w)
    offs = np.concatenate([[0], np.cumsum(widths)])
    c_q, c_k, c_v, c_h, c_bg, c_cg, c_mq, c_gd, c_gc, c_gm = [slice(int(a), int(b)) for a, b in zip(offs[:-1], offs[1:])]

    far_bucket = _t5_bucket(jnp.array([-(blk + 1)], jnp.int32)).astype(jnp.int32)
    bias_p = _bias_tiles(rel_bias_table, _prompt_buckets(blk), far_bucket, True)
    bias_s = _bias_tiles(rel_bias_table, _sample_buckets(past, ss), far_bucket, False)

    xp = x_prompt.reshape(bp * sp, d_model)
    xs = x_sample.reshape(bs * ss, d_model)
    outs = {k: [] for k in ("kp", "vp", "cp", "mkp", "mvp", "ks", "vs", "cs")}

    for l in range(depth):
        lam_init = 0.8 - 0.6 * math.exp(-0.3 * l)
        wl = w_in[l]
        w_qk = jnp.concatenate([wl[:, c_q], wl[:, c_k]], axis=1).astype(bf16)
        w_v = jnp.concatenate([wl[:, c_v], wl[:, c_gd]], axis=1).astype(bf16)
        w_cv = jnp.concatenate([wl[:, c_h], wl[:, c_bg], wl[:, c_cg], wl[:, c_gc]], axis=1).astype(bf16)
        w_mq = jnp.concatenate([wl[:, c_mq], wl[:, c_gm]], axis=1).astype(bf16)
        w_o = w_out[l].astype(bf16)
        w_kv = w_mem_kv[l].astype(bf16)
        lamp = jnp.stack([lambda_q1[l], lambda_k1[l], lambda_q2[l], lambda_k2[l]]).astype(f32)
        gscaled = subln_g[l].astype(f32) * (1.0 - lam_init)
        gcol = jnp.broadcast_to(gscaled[:, None], (HEAD_W, blk))
        grow = gscaled[None, :]
        g_ln = ln_g[l][None, :].astype(f32)
        b_ln = ln_b[l][None, :].astype(f32)
        cw = conv_w[l].astype(f32)

        q, kb, kf = _proj_qk(xp, w_qk, ROW_TILE)
        vt, vf, sgd = _proj_v(xp, w_v, ROW_TILE, bp, True)
        zc, utail = _proj_conv(xp, w_cv, cw, jnp.zeros((8, conv_ch), f32), ROW_TILE, sp)
        mq, sgm = _proj_mem(xp, w_mq, ROW_TILE, mem_dh)
        mk, mv = _memkv(mem_prompt.reshape(bp * n_mem, d_model), w_kv)
        zm = _mem_attn(mq, sgm, mk.reshape(bp, n_mem, mem_w), mv.reshape(bp, n_mem, mem_w), ROW_TILE, mem_dh, bp)
        zd = _attn_prompt(q.reshape(bp, sp, diff_w), kb.reshape(bp, sp, diff_w), vt,
                          sgd.reshape(bp, sp, diff_w), bias_p, lamp, gcol, lam_init)
        outs["kp"].append(kf.reshape(bp, sp, DIFF_H, HEAD_W))
        outs["vp"].append(vf.reshape(bp, sp, DIFF_H, HEAD_W))
        outs["cp"].append(utail[:, 8 - (CONV_K - 1):, :])
        outs["mkp"].append(mk.reshape(bp, n_mem, MEM_H, mem_dh))
        outs["mvp"].append(mv.reshape(bp, n_mem, MEM_H, mem_dh))
        xp = _out_proj(zd.reshape(bp * sp, diff_w), zc, zm, xp, w_o, g_ln, b_ln, ROW_TILE, alpha)

        q, kb, kf = _proj_qk(xs, w_qk, ROW_TILE)
        vb, vf, sgd = _proj_v(xs, w_v, ROW_TILE, bs, False)
        conv_init = jnp.pad(cache_conv[l].astype(f32), ((0, 0), (0, ss - (CONV_K - 1)), (0, 0)))
        zc, u = _proj_conv(xs, w_cv, cw, conv_init.reshape(bs * ss, conv_ch), ROW_TILE, ss)
        mq, sgm = _proj_mem(xs, w_mq, ROW_TILE, mem_dh)
        zm = _mem_attn(mq, sgm, cache_mem_k.reshape(depth * bs, n_mem, mem_w),
                       cache_mem_v.reshape(depth * bs, n_mem, mem_w), ss, mem_dh, bs, l * bs)
        zd = _attn_sample(q.reshape(bs, ss, diff_w), kb.reshape(bs, ss, diff_w), vb.reshape(bs, ss, diff_w),
                          cache_diff_k.reshape(depth * bs, past * DIFF_H, HEAD_W),
                          cache_diff_v.reshape(depth * bs, past * DIFF_H, HEAD_W),
                          sgd.reshape(bs, ss, diff_w), bias_s, lamp, grow, lam_init, l * bs)
        outs["ks"].append(kf.reshape(bs, ss, DIFF_H, HEAD_W))
        outs["vs"].append(vf.reshape(bs, ss, DIFF_H, HEAD_W))
        outs["cs"].append(u.reshape(bs, ss, conv_ch)[:, ss - (CONV_K - 1):, :])
        xs = _out_proj(zd.reshape(bs * ss, diff_w), zc, zm, xs, w_o, g_ln, b_ln, ROW_TILE, alpha)

    return (xp.reshape(bp, sp, d_model), xs.reshape(bs, ss, d_model),
            jnp.stack(outs["kp"]), jnp.stack(outs["vp"]), jnp.stack(outs["cp"]), jnp.stack(outs["mkp"]),
            jnp.stack(outs["mvp"]), jnp.stack(outs["ks"]), jnp.stack(outs["vs"]), jnp.stack(outs["cs"]))
```

```python
import functools
import math

import jax
import jax.numpy as jnp
import numpy as np
from jax import lax
from jax.experimental import pallas as pl
from jax.experimental.pallas import tpu as pltpu

f32 = jnp.float32
bf16 = jnp.bfloat16

CHUNK = 64
DIFF_H = 8
DIFF_DH = 64
HEAD_W = 2 * DIFF_DH
MEM_H = 4
CONV_K = 3
N_BUCKETS = 32
MAX_DISTANCE = 128
EPS = 1e-5

LOG2E = 1.4426950408889634
NEG_BIG = -1e30
MASKED_BUCKET = N_BUCKETS

ROW_TILE = 512
ATTN_BLOCK = 512
FAR_UNROLL = 2
SUM_ROWS = 16
VMEM_LIMIT = 56 * 1024 * 1024


def _params(n_axes):
    return pltpu.CompilerParams(dimension_semantics=("arbitrary",) * n_axes,
                                vmem_limit_bytes=VMEM_LIMIT)


def _silu(x):
    return x / (1.0 + jnp.exp(-x))


def _t5_bucket(rel):
    half = N_BUCKETS // 2
    max_exact = half // 2
    ret = jnp.where(rel > 0, half, 0)
    n = jnp.abs(rel)
    nf = jnp.maximum(n, 1).astype(f32)
    large = max_exact + (jnp.log(nf / max_exact) / math.log(MAX_DISTANCE / max_exact)
                         * (half - max_exact)).astype(jnp.int32)
    large = jnp.minimum(large, half - 1)
    return ret + jnp.where(n < max_exact, n, large)


def _bias_kernel(tab_ref, far_ref, bkt_ref, out_ref, *, subtract_far):
    h = pl.program_id(0)
    bkt = bkt_ref[...]
    val = jnp.zeros(bkt.shape, f32)
    for b in range(N_BUCKETS):
        val = jnp.where(bkt == b, tab_ref[b, h], val)
    if subtract_far:
        val = val - tab_ref[far_ref[0], h]
    out_ref[0] = jnp.where(bkt == MASKED_BUCKET, NEG_BIG, val * LOG2E)


def _bias_tiles(table, buckets, far_bucket, subtract_far):
    r, c = buckets.shape
    return pl.pallas_call(
        functools.partial(_bias_kernel, subtract_far=subtract_far),
        grid=(DIFF_H,),
        in_specs=[pl.BlockSpec(memory_space=pltpu.SMEM),
                  pl.BlockSpec(memory_space=pltpu.SMEM),
                  pl.BlockSpec((r, c), lambda h: (0, 0))],
        out_specs=pl.BlockSpec((1, r, c), lambda h: (h, 0, 0)),
        out_shape=jax.ShapeDtypeStruct((DIFF_H, r, c), f32),
        compiler_params=_params(1),
        name="bias_tiles",
    )(table, far_bucket, buckets)


def _prompt_buckets(blk):
    kk = jnp.arange(2 * blk)[:, None] - blk
    qq = jnp.arange(blk)[None, :]
    bkt = _t5_bucket(kk - qq)
    visible = (kk // CHUNK) <= (qq // CHUNK)
    return jnp.where(visible, bkt, MASKED_BUCKET).astype(jnp.int32)


def _sample_buckets(past, ss):
    qpos = past + jnp.arange(ss)[:, None]
    kpos = jnp.arange(past + ss)[None, :]
    bkt = _t5_bucket(kpos - qpos)
    visible = (kpos // CHUNK) <= (qpos // CHUNK)
    bkt = jnp.where(visible, bkt, MASKED_BUCKET).astype(jnp.int32)
    return jnp.concatenate([bkt, bkt], axis=0)


def _dot(a, b):
    return jnp.dot(a, b, preferred_element_type=f32)


def _proj_qk_kernel(x_ref, w_ref, q_ref, kb_ref, kf_ref, *, width, q_scale):
    xb = x_ref[...].astype(bf16)
    q_ref[...] = (_dot(xb, w_ref[:, :width]) * q_scale).astype(bf16)
    k = _dot(xb, w_ref[:, width:])
    kf_ref[...] = k
    kb_ref[...] = k.astype(bf16)


def _proj_v_kernel(x_ref, w_ref, vb_ref, vf_ref, sg_ref, *, width, transpose_v):
    xb = x_ref[...].astype(bf16)
    v = _dot(xb, w_ref[:, :width])
    vf_ref[...] = v
    if transpose_v:
        vb_ref[0] = v.T.astype(bf16)
    else:
        vb_ref[...] = v.astype(bf16)
    sg_ref[...] = _silu(_dot(xb, w_ref[:, width:])).astype(bf16)


def _proj_conv_kernel(x_ref, w_ref, cw_ref, init_ref, z_ref, u_ref, carry_ref, *, width, seq_len):
    rows_n = x_ref.shape[0]
    xb = x_ref[...].astype(bf16)
    h = _dot(xb, w_ref[:, 0 * width:1 * width])
    bg = _dot(xb, w_ref[:, 1 * width:2 * width])
    cg = _dot(xb, w_ref[:, 2 * width:3 * width])
    gc = _dot(xb, w_ref[:, 3 * width:4 * width])
    u = cg * h
    rows = lax.broadcasted_iota(jnp.int32, u.shape, 0)
    if seq_len >= rows_n:
        tiles_per_seq = seq_len // rows_n
        @pl.when(pl.program_id(0) % tiles_per_seq == 0)
        def _():
            carry_ref[...] = init_ref[...]

        carry = carry_ref[...]
        prev1 = jnp.where(rows == 0, carry[7:8], pltpu.roll(u, 1, 0))
        prev2 = jnp.where(rows == 0, carry[6:7], jnp.where(rows == 1, carry[7:8], pltpu.roll(u, 2, 0)))
        carry_ref[...] = u[rows_n - 8:]
        u_ref[0] = u[rows_n - 8:]
    else:
        t = rows % seq_len
        init = init_ref[...]
        prev1 = jnp.where(t == 0, pltpu.roll(init, rows_n - 1, 0), pltpu.roll(u, 1, 0))
        prev2 = jnp.where(t < 2, init, pltpu.roll(u, 2, 0))
        u_ref[...] = u
    cw = cw_ref[...]
    y = cw[0:1] * prev2 + cw[1:2] * prev1 + cw[2:3] * u
    z_ref[...] = ((bg * y) * _silu(gc)).astype(bf16)


def _proj_mem_kernel(x_ref, w_ref, mq_ref, sg_ref, *, width, q_scale):
    xb = x_ref[...].astype(bf16)
    mq_ref[...] = (_dot(xb, w_ref[:, :width]) * q_scale).astype(bf16)
    sg_ref[...] = _silu(_dot(xb, w_ref[:, width:])).astype(bf16)


def _row_spec(tm, n):
    return pl.BlockSpec((tm, n), lambda m: (m, 0))


def _full_spec(shape):
    return pl.BlockSpec(shape, lambda m: (0,) * len(shape))


def _proj_qk(x, w, tm):
    rows, d = x.shape
    width = w.shape[1] // 2
    sds = jax.ShapeDtypeStruct
    return pl.pallas_call(
        functools.partial(_proj_qk_kernel, width=width, q_scale=DIFF_DH ** -0.5 * LOG2E),
        grid=(rows // tm,),
        in_specs=[_row_spec(tm, d), _full_spec(w.shape)],
        out_specs=[_row_spec(tm, width)] * 3,
        out_shape=[sds((rows, width), bf16), sds((rows, width), bf16), sds((rows, width), f32)],
        compiler_params=_params(1),
        name="proj_qk",
    )(x, w)


def _proj_v(x, w, tm, n_seq, transpose_v):
    rows, d = x.shape
    width = w.shape[1] // 2
    seq = rows // n_seq
    sds = jax.ShapeDtypeStruct
    if transpose_v:
        tiles = seq // tm
        vb_spec = pl.BlockSpec((1, width, tm), lambda m: (m // tiles, 0, m % tiles))
        vb_shape = sds((n_seq, width, seq), bf16)
    else:
        vb_spec = _row_spec(tm, width)
        vb_shape = sds((rows, width), bf16)
    return pl.pallas_call(
        functools.partial(_proj_v_kernel, width=width, transpose_v=transpose_v),
        grid=(rows // tm,),
        in_specs=[_row_spec(tm, d), _full_spec(w.shape)],
        out_specs=[vb_spec, _row_spec(tm, width), _row_spec(tm, width)],
        out_shape=[vb_shape, sds((rows, width), f32), sds((rows, width), bf16)],
        compiler_params=_params(1),
        name="proj_v",
    )(x, w)


def _proj_conv(x, w, conv_w, init, tm, seq_len):
    rows, d = x.shape
    width = w.shape[1] // 4
    sds = jax.ShapeDtypeStruct
    if seq_len >= tm:
        tiles = seq_len // tm
        u_spec = pl.BlockSpec((1, 8, width), lambda m: (m // tiles, 0, 0))
        u_shape = sds((rows // seq_len, 8, width), f32)
        init_spec = _full_spec(init.shape)
    else:
        u_spec = _row_spec(tm, width)
        u_shape = sds((rows, width), f32)
        init_spec = _row_spec(tm, width)
    return pl.pallas_call(
        functools.partial(_proj_conv_kernel, width=width, seq_len=seq_len),
        grid=(rows // tm,),
        in_specs=[_row_spec(tm, d), _full_spec(w.shape), _full_spec(conv_w.shape), init_spec],
        out_specs=[_row_spec(tm, width), u_spec],
        out_shape=[sds((rows, width), bf16), u_shape],
        scratch_shapes=[pltpu.VMEM((8, width), f32)],
        compiler_params=_params(1),
        name="proj_conv",
    )(x, w, conv_w, init)


def _proj_mem(x, w, tm, head_dim):
    rows, d = x.shape
    width = w.shape[1] // 2
    sds = jax.ShapeDtypeStruct
    return pl.pallas_call(
        functools.partial(_proj_mem_kernel, width=width, q_scale=head_dim ** -0.5 * LOG2E),
        grid=(rows // tm,),
        in_specs=[_row_spec(tm, d), _full_spec(w.shape)],
        out_specs=[_row_spec(tm, width)] * 2,
        out_shape=[sds((rows, width), bf16)] * 2,
        compiler_params=_params(1),
        name="proj_mem",
    )(x, w)


def _memkv_kernel(x_ref, w_ref, k_ref, v_ref, *, width):
    xb = x_ref[...].astype(bf16)
    k_ref[...] = _dot(xb, w_ref[:, :width])
    v_ref[...] = _dot(xb, w_ref[:, width:])


def _memkv(x, w):
    rows, d = x.shape
    width = w.shape[1] // 2
    sds = jax.ShapeDtypeStruct
    return pl.pallas_call(
        functools.partial(_memkv_kernel, width=width),
        grid=(1,),
        in_specs=[_full_spec(x.shape), _full_spec(w.shape)],
        out_specs=[_full_spec((rows, width))] * 2,
        out_shape=[sds((rows, width), f32)] * 2,
        compiler_params=_params(1),
        name="mem_kv",
    )(x, w)


def _mem_attn_kernel(q_ref, sg_ref, k_ref, v_ref, z_ref, *, head_dim):
    for h in range(q_ref.shape[1] // head_dim):
        cols = slice(h * head_dim, (h + 1) * head_dim)
        kh = k_ref[0, :, cols].astype(bf16)
        vh = v_ref[0, :, cols].astype(bf16)
        s = lax.dot_general(q_ref[:, cols], kh, (((1,), (1,)), ((), ())), preferred_element_type=f32)
        p = jnp.exp2(s - jnp.max(s, axis=-1, keepdims=True))
        o = _dot(p.astype(bf16), vh) / jnp.sum(p, axis=-1, keepdims=True)
        z_ref[:, cols] = (o * sg_ref[:, cols].astype(f32)).astype(bf16)


def _mem_attn(q, sg, mk, mv, tm, head_dim, n_seq, seq_base=0):
    rows, width = q.shape
    n_mem = mk.shape[1]
    tiles = (rows // n_seq) // tm
    kv_spec = pl.BlockSpec((1, n_mem, width), lambda m: (seq_base + m // tiles, 0, 0))
    return pl.pallas_call(
        functools.partial(_mem_attn_kernel, head_dim=head_dim),
        grid=(rows // tm,),
        in_specs=[_row_spec(tm, width), _row_spec(tm, width), kv_spec, kv_spec],
        out_specs=_row_spec(tm, width),
        out_shape=jax.ShapeDtypeStruct((rows, width), bf16),
        compiler_params=_params(1),
        name="mem_attn",
    )(q, sg, mk, mv)


def _lambda(lamp_ref, lam_init):
    lp = lamp_ref[...]
    s1 = jnp.sum(lp[0:1] * lp[1:2], axis=1, keepdims=True)
    s2 = jnp.sum(lp[2:3] * lp[3:4], axis=1, keepdims=True)
    return jnp.exp(s1) - jnp.exp(s2) + lam_init


def _two_maps(q):
    lane = lax.broadcasted_iota(jnp.int32, q.shape, 1)
    zero = jnp.zeros_like(q)
    return jnp.concatenate([jnp.where(lane < DIFF_DH, q, zero), jnp.where(lane >= DIFF_DH, q, zero)], axis=0)


def _attn_prompt_kernel(q_ref, k_ref, vt_ref, sg_ref, bias_ref, lamp_ref, gcol_ref, o_ref,
                        m_ref, mx_ref, alpha_ref, acc_ref, s_ref, p_ref, *, blk, lam_init):
    i = pl.program_id(2)
    qcat = _two_maps(q_ref[0])
    ones = jnp.ones((SUM_ROWS, blk), bf16)
    m_ref[...] = jnp.full(m_ref.shape, NEG_BIG, f32)
    acc_ref[...] = jnp.zeros(acc_ref.shape, f32)
    p_ref[...] = jnp.zeros(p_ref.shape, bf16)
    alpha_ref[...] = jnp.ones(alpha_ref.shape, f32)

    def put_scores(j, bias):
        kb = k_ref[0, pl.ds(pl.multiple_of(j * blk, blk), blk), :]
        st = lax.dot_general(kb, qcat, (((1,), (1,)), ((), ())), preferred_element_type=f32)
        if bias is not None:
            st = st + jnp.concatenate([bias, bias], axis=1)
        s_ref[...] = st
        mx_ref[...] = jnp.max(st, axis=0, keepdims=True)

    def weighted_values(j):
        vtb = jnp.concatenate([vt_ref[0, :, pl.ds(pl.multiple_of(j * blk, blk), blk)], ones], axis=0)
        return _dot(vtb, p_ref[...])[:acc_ref.shape[0]]

    def step(j, next_bias, prefetch):
        pv_prev = weighted_values(jnp.maximum(j - 1, 0))
        m_old = m_ref[...]
        m_new = jnp.maximum(m_old, mx_ref[...])
        p = jnp.exp2(s_ref[...] - m_new).astype(bf16)
        if prefetch:
            put_scores(j + 1, next_bias)
        acc_ref[...] = alpha_ref[...] * acc_ref[...] + pv_prev
        alpha_ref[...] = jnp.exp2(m_old - m_new)
        m_ref[...] = m_new
        p_ref[...] = p

    bias_prev = lambda: bias_ref[0, :blk, :]
    bias_diag = lambda: bias_ref[0, blk:, :]

    @pl.when(i == 0)
    def _():
        put_scores(0, bias_diag())

    @pl.when(i == 1)
    def _():
        put_scores(0, bias_prev())

    @pl.when(i >= 2)
    def _():
        put_scores(0, None)

    n_far = jnp.maximum(i - 2, 0)
    n_groups = n_far // FAR_UNROLL

    def far_group(g, carry):
        for u in range(FAR_UNROLL):
            step(g * FAR_UNROLL + u, None, True)
        return carry

    def far_step(j, carry):
        step(j, None, True)
        return carry

    lax.fori_loop(0, n_groups, far_group, 0)
    lax.fori_loop(n_groups * FAR_UNROLL, n_far, far_step, 0)

    @pl.when(i >= 2)
    def _():
        step(i - 2, bias_prev(), True)

    @pl.when(i >= 1)
    def _():
        step(i - 1, bias_diag(), True)

    step(i, None, False)
    acc_ref[...] = alpha_ref[...] * acc_ref[...] + weighted_values(i)

    lam = _lambda(lamp_ref, lam_init)
    ot = acc_ref[:HEAD_W, :] * (1.0 / acc_ref[HEAD_W:HEAD_W + 1, :])
    d = ot[:, :blk] - lam * ot[:, blk:]
    ms = jnp.mean(d * d, axis=0, keepdims=True)
    y = d * lax.rsqrt(ms + EPS) * gcol_ref[...]
    o_ref[0] = (y.T * sg_ref[0].astype(f32)).astype(bf16)


def _attn_prompt(q, k, vt, sg, bias, lamp, gcol, lam_init):
    b, s, w = q.shape
    blk = ATTN_BLOCK
    qspec = pl.BlockSpec((1, blk, HEAD_W), lambda bi, h, i: (bi, i, h))
    return pl.pallas_call(
        functools.partial(_attn_prompt_kernel, blk=blk, lam_init=lam_init),
        grid=(b, DIFF_H, s // blk),
        in_specs=[qspec,
                  pl.BlockSpec((1, s, HEAD_W), lambda bi, h, i: (bi, 0, h)),
                  pl.BlockSpec((1, HEAD_W, s), lambda bi, h, i: (bi, h, 0)),
                  qspec,
                  pl.BlockSpec((1, 2 * blk, blk), lambda bi, h, i: (h, 0, 0)),
                  pl.BlockSpec(lamp.shape, lambda bi, h, i: (0, 0)),
                  pl.BlockSpec(gcol.shape, lambda bi, h, i: (0, 0))],
        out_specs=qspec,
        out_shape=jax.ShapeDtypeStruct((b, s, w), bf16),
        scratch_shapes=[pltpu.VMEM((1, 2 * blk), f32)] * 3 + [
                        pltpu.VMEM((HEAD_W + 8, 2 * blk), f32),
                        pltpu.VMEM((blk, 2 * blk), f32), pltpu.VMEM((blk, 2 * blk), bf16)],
        compiler_params=_params(3),
        name="attn_prompt",
    )(q, k, vt, sg, bias, lamp, gcol)


def _attn_sample_kernel(q_ref, kn_ref, vn_ref, kc_ref, vc_ref, sg_ref, bias_ref, lamp_ref, grow_ref, o_ref,
                        *, past, lam_init):
    ss = q_ref.shape[1]
    lam = _lambda(lamp_ref, lam_init)
    nt = (((1,), (1,)), ((), ()))
    for h in range(DIFF_H):
        cols = slice(h * HEAD_W, (h + 1) * HEAD_W)
        qcat = _two_maps(q_ref[0, :, cols])
        head_rows = pl.ds(h, past, stride=DIFF_H)
        sc = lax.dot_general(qcat, kc_ref[0, head_rows, :].astype(bf16), nt, preferred_element_type=f32)
        sn = lax.dot_general(qcat, kn_ref[0, :, cols], nt, preferred_element_type=f32)
        sc = sc + bias_ref[h, :, :past]
        sn = sn + bias_ref[h, :, past:]
        m = jnp.maximum(jnp.max(sc, axis=-1, keepdims=True), jnp.max(sn, axis=-1, keepdims=True))
        pc = jnp.exp2(sc - m)
        pn = jnp.exp2(sn - m)
        l = jnp.sum(pc, axis=-1, keepdims=True) + jnp.sum(pn, axis=-1, keepdims=True)
        o = (_dot(pc.astype(bf16), vc_ref[0, head_rows, :].astype(bf16)) + _dot(pn.astype(bf16), vn_ref[0, :, cols])) / l
        d = o[:ss] - lam * o[ss:]
        ms = jnp.mean(d * d, axis=-1, keepdims=True)
        y = d * lax.rsqrt(ms + EPS) * grow_ref[...]
        o_ref[0, :, cols] = (y * sg_ref[0, :, cols].astype(f32)).astype(bf16)


def _attn_sample(q, kn, vn, kc, vc, sg, bias, lamp, grow, lam_init, cache_base):
    b, ss, w = q.shape
    past = kc.shape[1] // DIFF_H
    new_spec = pl.BlockSpec((1, ss, w), lambda bi: (bi, 0, 0))
    cache_spec = pl.BlockSpec((1, past * DIFF_H, HEAD_W), lambda bi: (cache_base + bi, 0, 0))
    return pl.pallas_call(
        functools.partial(_attn_sample_kernel, past=past, lam_init=lam_init),
        grid=(b,),
        in_specs=[new_spec, new_spec, new_spec, cache_spec, cache_spec, new_spec,
                  _full_spec(bias.shape), _full_spec(lamp.shape), _full_spec(grow.shape)],
        out_specs=new_spec,
        out_shape=jax.ShapeDtypeStruct((b, ss, w), bf16),
        compiler_params=_params(1),
        name="attn_sample",
    )(q, kn, vn, kc, vc, sg, bias, lamp, grow)


def _out_kernel(zd_ref, zc_ref, zm_ref, x_ref, w_ref, g_ref, b_ref, y_ref, *, alpha):
    wd = zd_ref.shape[1]
    wc = zc_ref.shape[1]
    o = (_dot(zd_ref[...], w_ref[:wd]) + _dot(zc_ref[...], w_ref[wd:wd + wc])
         + _dot(zm_ref[...], w_ref[wd + wc:]))
    r = alpha * x_ref[...] + o
    mu = jnp.mean(r, axis=-1, keepdims=True)
    c = r - mu
    var = jnp.mean(c * c, axis=-1, keepdims=True)
    y_ref[...] = c * lax.rsqrt(var + EPS) * g_ref[...] + b_ref[...]


def _out_proj(zd, zc, zm, x, w, g, b, tm, alpha):
    rows, d = x.shape
    return pl.pallas_call(
        functools.partial(_out_kernel, alpha=alpha),
        grid=(rows // tm,),
        in_specs=[_row_spec(tm, zd.shape[1]), _row_spec(tm, zc.shape[1]), _row_spec(tm, zm.shape[1]),
                  _row_spec(tm, d), _full_spec(w.shape), _full_spec(g.shape), _full_spec(b.shape)],
        out_specs=_row_spec(tm, d),
        out_shape=jax.ShapeDtypeStruct((rows, d), f32),
        compiler_params=_params(1),
        name="out_proj",
    )(zd, zc, zm, x, w, g, b)


def kernel(x_prompt, x_sample, cache_diff_k, cache_diff_v, cache_conv, cache_mem_k, cache_mem_v, mem_prompt,
           rel_bias_table, w_in, w_mem_kv, conv_w, lambda_q1, lambda_k1, lambda_q2, lambda_k2, subln_g, w_out,
           ln_g, ln_b):
    depth = w_in.shape[0]
    bp, sp, d_model = x_prompt.shape
    bs, ss, _ = x_sample.shape
    past = cache_diff_k.shape[2]
    n_mem = mem_prompt.shape[1]
    diff_w = DIFF_H * HEAD_W
    conv_ch = conv_w.shape[2]
    mem_w = cache_mem_k.shape[3] * cache_mem_k.shape[4]
    mem_dh = cache_mem_k.shape[4]
    alpha = (2 * depth) ** 0.25
    blk = ATTN_BLOCK
    assert sp % blk == 0 and blk % CHUNK == 0 and sp % ROW_TILE == 0 and (bs * ss) % ROW_TILE == 0
    assert conv_w.shape[1] == CONV_K and ss >= CONV_K - 1

    widths = (diff_w, diff_w, diff_w, conv_ch, conv_ch, conv_ch, mem_w, diff_w, conv_ch, mem_w)
    offs = np.concatenate([[0], np.cumsum(widths)])
    c_q, c_k, c_v, c_h, c_bg, c_cg, c_mq, c_gd, c_gc, c_gm = [slice(int(a), int(b)) for a, b in zip(offs[:-1], offs[1:])]

    far_bucket = _t5_bucket(jnp.array([-(blk + 1)], jnp.int32)).astype(jnp.int32)
    bias_p = _bias_tiles(rel_bias_table, _prompt_buckets(blk), far_bucket, True)
    bias_s = _bias_tiles(rel_bias_table, _sample_buckets(past, ss), far_bucket, False)

    xp = x_prompt.reshape(bp * sp, d_model)
    xs = x_sample.reshape(bs * ss, d_model)
    outs = {k: [] for k in ("kp", "vp", "cp", "mkp", "mvp", "ks", "vs", "cs")}

    for l in range(depth):
        lam_init = 0.8 - 0.6 * math.exp(-0.3 * l)
        wl = w_in[l]
        w_qk = jnp.concatenate([wl[:, c_q], wl[:, c_k]], axis=1).astype(bf16)
        w_v = jnp.concatenate([wl[:, c_v], wl[:, c_gd]], axis=1).astype(bf16)
        w_cv = jnp.concatenate([wl[:, c_h], wl[:, c_bg], wl[:, c_cg], wl[:, c_gc]], axis=1).astype(bf16)
        w_mq = jnp.concatenate([wl[:, c_mq], wl[:, c_gm]], axis=1).astype(bf16)
        w_o = w_out[l].astype(bf16)
        w_kv = w_mem_kv[l].astype(bf16)
        lamp = jnp.stack([lambda_q1[l], lambda_k1[l], lambda_q2[l], lambda_k2[l]]).astype(f32)
        gscaled = subln_g[l].astype(f32) * (1.0 - lam_init)
        gcol = jnp.broadcast_to(gscaled[:, None], (HEAD_W, blk))
        grow = gscaled[None, :]
        g_ln = ln_g[l][None, :].astype(f32)
        b_ln = ln_b[l][None, :].astype(f32)
        cw = conv_w[l].astype(f32)

        q, kb, kf = _proj_qk(xp, w_qk, ROW_TILE)
        vt, vf, sgd = _proj_v(xp, w_v, ROW_TILE, bp, True)
        zc, utail = _proj_conv(xp, w_cv, cw, jnp.zeros((8, conv_ch), f32), ROW_TILE, sp)
        mq, sgm = _proj_mem(xp, w_mq, ROW_TILE, mem_dh)
        mk, mv = _memkv(mem_prompt.reshape(bp * n_mem, d_model), w_kv)
        zm = _mem_attn(mq, sgm, mk.reshape(bp, n_mem, mem_w), mv.reshape(bp, n_mem, mem_w), ROW_TILE, mem_dh, bp)
        zd = _attn_prompt(q.reshape(bp, sp, diff_w), kb.reshape(bp, sp, diff_w), vt,
                          sgd.reshape(bp, sp, diff_w), bias_p, lamp, gcol, lam_init)
        outs["kp"].append(kf.reshape(bp, sp, DIFF_H, HEAD_W))
        outs["vp"].append(vf.reshape(bp, sp, DIFF_H, HEAD_W))
        outs["cp"].append(utail[:, 8 - (CONV_K - 1):, :])
        outs["mkp"].append(mk.reshape(bp, n_mem, MEM_H, mem_dh))
        outs["mvp"].append(mv.reshape(bp, n_mem, MEM_H, mem_dh))
        xp = _out_proj(zd.reshape(bp * sp, diff_w), zc, zm, xp, w_o, g_ln, b_ln, ROW_TILE, alpha)

        q, kb, kf = _proj_qk(xs, w_qk, ROW_TILE)
        vb, vf, sgd = _proj_v(xs, w_v, ROW_TILE, bs, False)
        conv_init = jnp.pad(cache_conv[l].astype(f32), ((0, 0), (0, ss - (CONV_K - 1)), (0, 0)))
        zc, u = _proj_conv(xs, w_cv, cw, conv_init.reshape(bs * ss, conv_ch), ROW_TILE, ss)
        mq, sgm = _proj_mem(xs, w_mq, ROW_TILE, mem_dh)
        zm = _mem_attn(mq, sgm, cache_mem_k.reshape(depth * bs, n_mem, mem_w),
                       cache_mem_v.reshape(depth * bs, n_mem, mem_w), ss, mem_dh, bs, l * bs)
        zd = _attn_sample(q.reshape(bs, ss, diff_w), kb.reshape(bs, ss, diff_w), vb.reshape(bs, ss, diff_w),
                          cache_diff_k.reshape(depth * bs, past * DIFF_H, HEAD_W),
                          cache_diff_v.reshape(depth * bs, past * DIFF_H, HEAD_W),
                          sgd.reshape(bs, ss, diff_w), bias_s, lamp, grow, lam_init, l * bs)
        outs["ks"].append(kf.reshape(bs, ss, DIFF_H, HEAD_W))
        outs["vs"].append(vf.reshape(bs, ss, DIFF_H, HEAD_W))
        outs["cs"].append(u.reshape(bs, ss, conv_ch)[:, ss - (CONV_K - 1):, :])
        xs = _out_proj(zd.reshape(bs * ss, diff_w), zc, zm, xs, w_o, g_ln, b_ln, ROW_TILE, alpha)

    return (xp.reshape(bp, sp, d_model), xs.reshape(bs, ss, d_model),
            jnp.stack(outs["kp"]), jnp.stack(outs["vp"]), jnp.stack(outs["cp"]), jnp.stack(outs["mkp"]),
            jnp.stack(outs["mvp"]), jnp.stack(outs["ks"]), jnp.stack(outs["vs"]), jnp.stack(outs["cs"]))
```

```python
import functools
import math

import jax
import jax.numpy as jnp
import numpy as np
from jax import lax
from jax.experimental import pallas as pl
from jax.experimental.pallas import tpu as pltpu

f32 = jnp.float32
bf16 = jnp.bfloat16

CHUNK = 64
DIFF_H = 8
DIFF_DH = 64
HEAD_W = 2 * DIFF_DH
MEM_H = 4
CONV_K = 3
N_BUCKETS = 32
MAX_DISTANCE = 128
EPS = 1e-5

LOG2E = 1.4426950408889634
NEG_BIG = -1e30
MASKED_BUCKET = N_BUCKETS

ROW_TILE = 512
ATTN_BLOCK = 512
FAR_UNROLL = 2
SUM_ROWS = 16
VMEM_LIMIT = 56 * 1024 * 1024


def _params(n_axes):
    return pltpu.CompilerParams(dimension_semantics=("arbitrary",) * n_axes,
                                vmem_limit_bytes=VMEM_LIMIT)


def _silu(x):
    return x / (1.0 + jnp.exp(-x))


def _t5_bucket(rel):
    half = N_BUCKETS // 2
    max_exact = half // 2
    ret = jnp.where(rel > 0, half, 0)
    n = jnp.abs(rel)
    nf = jnp.maximum(n, 1).astype(f32)
    large = max_exact + (jnp.log(nf / max_exact) / math.log(MAX_DISTANCE / max_exact)
                         * (half - max_exact)).astype(jnp.int32)
    large = jnp.minimum(large, half - 1)
    return ret + jnp.where(n < max_exact, n, large)


def _bias_kernel(tab_ref, far_ref, bkt_ref, out_ref, *, subtract_far):
    h = pl.program_id(0)
    bkt = bkt_ref[...]
    val = jnp.zeros(bkt.shape, f32)
    for b in range(N_BUCKETS):
        val = jnp.where(bkt == b, tab_ref[b, h], val)
    if subtract_far:
        val = val - tab_ref[far_ref[0], h]
    out_ref[0] = jnp.where(bkt == MASKED_BUCKET, NEG_BIG, val * LOG2E)


def _bias_tiles(table, buckets, far_bucket, subtract_far):
    r, c = buckets.shape
    return pl.pallas_call(
        functools.partial(_bias_kernel, subtract_far=subtract_far),
        grid=(DIFF_H,),
        in_specs=[pl.BlockSpec(memory_space=pltpu.SMEM),
                  pl.BlockSpec(memory_space=pltpu.SMEM),
                  pl.BlockSpec((r, c), lambda h: (0, 0))],
        out_specs=pl.BlockSpec((1, r, c), lambda h: (h, 0, 0)),
        out_shape=jax.ShapeDtypeStruct((DIFF_H, r, c), f32),
        compiler_params=_params(1),
        name="bias_tiles",
    )(table, far_bucket, buckets)


def _prompt_buckets(blk):
    kk = jnp.arange(2 * blk)[:, None] - blk
    qq = jnp.arange(blk)[None, :]
    bkt = _t5_bucket(kk - qq)
    visible = (kk // CHUNK) <= (qq // CHUNK)
    return jnp.where(visible, bkt, MASKED_BUCKET).astype(jnp.int32)


def _sample_buckets(past, ss):
    qpos = past + jnp.arange(ss)[:, None]
    kpos = jnp.arange(past + ss)[None, :]
    bkt = _t5_bucket(kpos - qpos)
    visible = (kpos // CHUNK) <= (qpos // CHUNK)
    bkt = jnp.where(visible, bkt, MASKED_BUCKET).astype(jnp.int32)
    return jnp.concatenate([bkt, bkt], axis=0)


def _dot(a, b):
    return jnp.dot(a, b, preferred_element_type=f32)


def _proj_qk_kernel(x_ref, w_ref, q_ref, kb_ref, kf_ref, *, width, q_scale):
    xb = x_ref[...].astype(bf16)
    q_ref[...] = (_dot(xb, w_ref[:, :width]) * q_scale).astype(bf16)
    k = _dot(xb, w_ref[:, width:])
    kf_ref[...] = k
    kb_ref[...] = k.astype(bf16)


def _proj_v_kernel(x_ref, w_ref, vb_ref, vf_ref, sg_ref, *, width, transpose_v):
    xb = x_ref[...].astype(bf16)
    v = _dot(xb, w_ref[:, :width])
    vf_ref[...] = v
    if transpose_v:
        vb_ref[0] = v.T.astype(bf16)
    else:
        vb_ref[...] = v.astype(bf16)
    sg_ref[...] = _silu(_dot(xb, w_ref[:, width:])).astype(bf16)


def _proj_conv_kernel(x_ref, w_ref, cw_ref, init_ref, z_ref, u_ref, carry_ref, *, width, seq_len):
    rows_n = x_ref.shape[0]
    xb = x_ref[...].astype(bf16)
    h = _dot(xb, w_ref[:, 0 * width:1 * width])
    bg = _dot(xb, w_ref[:, 1 * width:2 * width])
    cg = _dot(xb, w_ref[:, 2 * width:3 * width])
    gc = _dot(xb, w_ref[:, 3 * width:4 * width])
    u = cg * h
    rows = lax.broadcasted_iota(jnp.int32, u.shape, 0)
    if seq_len >= rows_n:
        tiles_per_seq = seq_len // rows_n
        @pl.when(pl.program_id(0) % tiles_per_seq == 0)
        def _():
            carry_ref[...] = init_ref[...]

        carry = carry_ref[...]
        prev1 = jnp.where(rows == 0, carry[7:8], pltpu.roll(u, 1, 0))
        prev2 = jnp.where(rows == 0, carry[6:7], jnp.where(rows == 1, carry[7:8], pltpu.roll(u, 2, 0)))
        carry_ref[...] = u[rows_n - 8:]
        u_ref[0] = u[rows_n - 8:]
    else:
        t = rows % seq_len
        init = init_ref[...]
        prev1 = jnp.where(t == 0, pltpu.roll(init, rows_n - 1, 0), pltpu.roll(u, 1, 0))
        prev2 = jnp.where(t < 2, init, pltpu.roll(u, 2, 0))
        u_ref[...] = u
    cw = cw_ref[...]
    y = cw[0:1] * prev2 + cw[1:2] * prev1 + cw[2:3] * u
    z_ref[...] = ((bg * y) * _silu(gc)).astype(bf16)


def _proj_mem_kernel(x_ref, w_ref, mq_ref, sg_ref, *, width, q_scale):
    xb = x_ref[...].astype(bf16)
    mq_ref[...] = (_dot(xb, w_ref[:, :width]) * q_scale).astype(bf16)
    sg_ref[...] = _silu(_dot(xb, w_ref[:, width:])).astype(bf16)


def _row_spec(tm, n):
    return pl.BlockSpec((tm, n), lambda m: (m, 0))


def _full_spec(shape):
    return pl.BlockSpec(shape, lambda m: (0,) * len(shape))


def _proj_qk(x, w, tm):
    rows, d = x.shape
    width = w.shape[1] // 2
    sds = jax.ShapeDtypeStruct
    return pl.pallas_call(
        functools.partial(_proj_qk_kernel, width=width, q_scale=DIFF_DH ** -0.5 * LOG2E),
        grid=(rows // tm,),
        in_specs=[_row_spec(tm, d), _full_spec(w.shape)],
        out_specs=[_row_spec(tm, width)] * 3,
        out_shape=[sds((rows, width), bf16), sds((rows, width), bf16), sds((rows, width), f32)],
        compiler_params=_params(1),
        name="proj_qk",
    )(x, w)


def _proj_v(x, w, tm, n_seq, transpose_v):
    rows, d = x.shape
    width = w.shape[1] // 2
    seq = rows // n_seq
    sds = jax.ShapeDtypeStruct
    if transpose_v:
        tiles = seq // tm
        vb_spec = pl.BlockSpec((1, width, tm), lambda m: (m // tiles, 0, m % tiles))
        vb_shape = sds((n_seq, width, seq), bf16)
    else:
        vb_spec = _row_spec(tm, width)
        vb_shape = sds((rows, width), bf16)
    return pl.pallas_call(
        functools.partial(_proj_v_kernel, width=width, transpose_v=transpose_v),
        grid=(rows // tm,),
        in_specs=[_row_spec(tm, d), _full_spec(w.shape)],
        out_specs=[vb_spec, _row_spec(tm, width), _row_spec(tm, width)],
        out_shape=[vb_shape, sds((rows, width), f32), sds((rows, width), bf16)],
        compiler_params=_params(1),
        name="proj_v",
    )(x, w)


def _proj_conv(x, w, conv_w, init, tm, seq_len):
    rows, d = x.shape
    width = w.shape[1] // 4
    sds = jax.ShapeDtypeStruct
    if seq_len >= tm:
        tiles = seq_len // tm
        u_spec = pl.BlockSpec((1, 8, width), lambda m: (m // tiles, 0, 0))
        u_shape = sds((rows // seq_len, 8, width), f32)
        init_spec = _full_spec(init.shape)
    else:
        u_spec = _row_spec(tm, width)
        u_shape = sds((rows, width), f32)
        init_spec = _row_spec(tm, width)
    return pl.pallas_call(
        functools.partial(_proj_conv_kernel, width=width, seq_len=seq_len),
        grid=(rows // tm,),
        in_specs=[_row_spec(tm, d), _full_spec(w.shape), _full_spec(conv_w.shape), init_spec],
        out_specs=[_row_spec(tm, width), u_spec],
        out_shape=[sds((rows, width), bf16), u_shape],
        scratch_shapes=[pltpu.VMEM((8, width), f32)],
        compiler_params=_params(1),
        name="proj_conv",
    )(x, w, conv_w, init)


def _proj_mem(x, w, tm, head_dim):
    rows, d = x.shape
    width = w.shape[1] // 2
    sds = jax.ShapeDtypeStruct
    return pl.pallas_call(
        functools.partial(_proj_mem_kernel, width=width, q_scale=head_dim ** -0.5 * LOG2E),
        grid=(rows // tm,),
        in_specs=[_row_spec(tm, d), _full_spec(w.shape)],
        out_specs=[_row_spec(tm, width)] * 2,
        out_shape=[sds((rows, width), bf16)] * 2,
        compiler_params=_params(1),
        name="proj_mem",
    )(x, w)


def _memkv_kernel(x_ref, w_ref, k_ref, v_ref, *, width):
    xb = x_ref[...].astype(bf16)
    k_ref[...] = _dot(xb, w_ref[:, :width])
    v_ref[...] = _dot(xb, w_ref[:, width:])


def _memkv(x, w):
    rows, d = x.shape
    width = w.shape[1] // 2
    sds = jax.ShapeDtypeStruct
    return pl.pallas_call(
        functools.partial(_memkv_kernel, width=width),
        grid=(1,),
        in_specs=[_full_spec(x.shape), _full_spec(w.shape)],
        out_specs=[_full_spec((rows, width))] * 2,
        out_shape=[sds((rows, width), f32)] * 2,
        compiler_params=_params(1),
        name="mem_kv",
    )(x, w)


def _mem_attn_kernel(q_ref, sg_ref, k_ref, v_ref, z_ref, *, head_dim):
    for h in range(q_ref.shape[1] // head_dim):
        cols = slice(h * head_dim, (h + 1) * head_dim)
        kh = k_ref[0, :, cols].astype(bf16)
        vh = v_ref[0, :, cols].astype(bf16)
        s = lax.dot_general(q_ref[:, cols], kh, (((1,), (1,)), ((), ())), preferred_element_type=f32)
        p = jnp.exp2(s - jnp.max(s, axis=-1, keepdims=True))
        o = _dot(p.astype(bf16), vh) / jnp.sum(p, axis=-1, keepdims=True)
        z_ref[:, cols] = (o * sg_ref[:, cols].astype(f32)).astype(bf16)


def _mem_attn(q, sg, mk, mv, tm, head_dim, n_seq, seq_base=0):
    rows, width = q.shape
    n_mem = mk.shape[1]
    tiles = (rows // n_seq) // tm
    kv_spec = pl.BlockSpec((1, n_mem, width), lambda m: (seq_base + m // tiles, 0, 0))
    return pl.pallas_call(
        functools.partial(_mem_attn_kernel, head_dim=head_dim),
        grid=(rows // tm,),
        in_specs=[_row_spec(tm, width), _row_spec(tm, width), kv_spec, kv_spec],
        out_specs=_row_spec(tm, width),
        out_shape=jax.ShapeDtypeStruct((rows, width), bf16),
        compiler_params=_params(1),
        name="mem_attn",
    )(q, sg, mk, mv)


def _lambda(lamp_ref, lam_init):
    lp = lamp_ref[...]
    s1 = jnp.sum(lp[0:1] * lp[1:2], axis=1, keepdims=True)
    s2 = jnp.sum(lp[2:3] * lp[3:4], axis=1, keepdims=True)
    return jnp.exp(s1) - jnp.exp(s2) + lam_init


def _two_maps(q):
    lane = lax.broadcasted_iota(jnp.int32, q.shape, 1)
    zero = jnp.zeros_like(q)
    return jnp.concatenate([jnp.where(lane < DIFF_DH, q, zero), jnp.where(lane >= DIFF_DH, q, zero)], axis=0)


def _attn_prompt_kernel(q_ref, k_ref, vt_ref, sg_ref, bias_ref, lamp_ref, gcol_ref, o_ref,
                        m_ref, mx_ref, alpha_ref, acc_ref, s_ref, p_ref, *, blk, lam_init):
    i = pl.program_id(2)
    qcat = _two_maps(q_ref[0])
    ones = jnp.ones((SUM_ROWS, blk), bf16)
    m_ref[...] = jnp.full(m_ref.shape, NEG_BIG, f32)
    acc_ref[...] = jnp.zeros(acc_ref.shape, f32)
    p_ref[...] = jnp.zeros(p_ref.shape, bf16)
    alpha_ref[...] = jnp.ones(alpha_ref.shape, f32)

    def put_scores(j, bias):
        kb = k_ref[0, pl.ds(pl.multiple_of(j * blk, blk), blk), :]
        st = lax.dot_general(kb, qcat, (((1,), (1,)), ((), ())), preferred_element_type=f32)
        if bias is not None:
            st = st + jnp.concatenate([bias, bias], axis=1)
        s_ref[...] = st
        mx_ref[...] = jnp.max(st, axis=0, keepdims=True)

    def weighted_values(j):
        vtb = jnp.concatenate([vt_ref[0, :, pl.ds(pl.multiple_of(j * blk, blk), blk)], ones], axis=0)
        return _dot(vtb, p_ref[...])[:acc_ref.shape[0]]

    def step(j, next_bias, prefetch):
        pv_prev = weighted_values(jnp.maximum(j - 1, 0))
        m_old = m_ref[...]
        m_new = jnp.maximum(m_old, mx_ref[...])
        p = jnp.exp2(s_ref[...] - m_new).astype(bf16)
        if prefetch:
            put_scores(j + 1, next_bias)
        acc_ref[...] = alpha_ref[...] * acc_ref[...] + pv_prev
        alpha_ref[...] = jnp.exp2(m_old - m_new)
        m_ref[...] = m_new
        p_ref[...] = p

    bias_prev = lambda: bias_ref[0, :blk, :]
    bias_diag = lambda: bias_ref[0, blk:, :]

    @pl.when(i == 0)
    def _():
        put_scores(0, bias_diag())

    @pl.when(i == 1)
    def _():
        put_scores(0, bias_prev())

    @pl.when(i >= 2)
    def _():
        put_scores(0, None)

    n_far = jnp.maximum(i - 2, 0)
    n_groups = n_far // FAR_UNROLL

    def far_group(g, carry):
        for u in range(FAR_UNROLL):
            step(g * FAR_UNROLL + u, None, True)
        return carry

    def far_step(j, carry):
        step(j, None, True)
        return carry

    lax.fori_loop(0, n_groups, far_group, 0)
    lax.fori_loop(n_groups * FAR_UNROLL, n_far, far_step, 0)

    @pl.when(i >= 2)
    def _():
        step(i - 2, bias_prev(), True)
        step(i - 1, bias_diag(), True)

    @pl.when(i == 1)
    def _():
        step(i - 1, bias_diag(), True)

    step(i, None, False)
    acc_ref[...] = alpha_ref[...] * acc_ref[...] + weighted_values(i)

    lam = _lambda(lamp_ref, lam_init)
    ot = acc_ref[:HEAD_W, :] * (1.0 / acc_ref[HEAD_W:HEAD_W + 1, :])
    d = ot[:, :blk] - lam * ot[:, blk:]
    ms = jnp.mean(d * d, axis=0, keepdims=True)
    y = d * lax.rsqrt(ms + EPS) * gcol_ref[...]
    o_ref[0] = (y.T * sg_ref[0].astype(f32)).astype(bf16)


def _attn_prompt(q, k, vt, sg, bias, lamp, gcol, lam_init):
    b, s, w = q.shape
    blk = ATTN_BLOCK
    qspec = pl.BlockSpec((1, blk, HEAD_W), lambda bi, h, i: (bi, i, h))
    return pl.pallas_call(
        functools.partial(_attn_prompt_kernel, blk=blk, lam_init=lam_init),
        grid=(b, DIFF_H, s // blk),
        in_specs=[qspec,
                  pl.BlockSpec((1, s, HEAD_W), lambda bi, h, i: (bi, 0, h)),
                  pl.BlockSpec((1, HEAD_W, s), lambda bi, h, i: (bi, h, 0)),
                  qspec,
                  pl.BlockSpec((1, 2 * blk, blk), lambda bi, h, i: (h, 0, 0)),
                  pl.BlockSpec(lamp.shape, lambda bi, h, i: (0, 0)),
                  pl.BlockSpec(gcol.shape, lambda bi, h, i: (0, 0))],
        out_specs=qspec,
        out_shape=jax.ShapeDtypeStruct((b, s, w), bf16),
        scratch_shapes=[pltpu.VMEM((1, 2 * blk), f32)] * 3 + [
                        pltpu.VMEM((HEAD_W + 8, 2 * blk), f32),
                        pltpu.VMEM((blk, 2 * blk), f32), pltpu.VMEM((blk, 2 * blk), bf16)],
        compiler_params=_params(3),
        name="attn_prompt",
    )(q, k, vt, sg, bias, lamp, gcol)


def _attn_sample_kernel(q_ref, kn_ref, vn_ref, kc_ref, vc_ref, sg_ref, bias_ref, lamp_ref, grow_ref, o_ref,
                        *, past, lam_init):
    ss = q_ref.shape[1]
    lam = _lambda(lamp_ref, lam_init)
    nt = (((1,), (1,)), ((), ()))
    for h in range(DIFF_H):
        cols = slice(h * HEAD_W, (h + 1) * HEAD_W)
        qcat = _two_maps(q_ref[0, :, cols])
        head_rows = pl.ds(h, past, stride=DIFF_H)
        sc = lax.dot_general(qcat, kc_ref[0, head_rows, :].astype(bf16), nt, preferred_element_type=f32)
        sn = lax.dot_general(qcat, kn_ref[0, :, cols], nt, preferred_element_type=f32)
        sc = sc + bias_ref[h, :, :past]
        sn = sn + bias_ref[h, :, past:]
        m = jnp.maximum(jnp.max(sc, axis=-1, keepdims=True), jnp.max(sn, axis=-1, keepdims=True))
        pc = jnp.exp2(sc - m)
        pn = jnp.exp2(sn - m)
        l = jnp.sum(pc, axis=-1, keepdims=True) + jnp.sum(pn, axis=-1, keepdims=True)
        o = (_dot(pc.astype(bf16), vc_ref[0, head_rows, :].astype(bf16)) + _dot(pn.astype(bf16), vn_ref[0, :, cols])) / l
        d = o[:ss] - lam * o[ss:]
        ms = jnp.mean(d * d, axis=-1, keepdims=True)
        y = d * lax.rsqrt(ms + EPS) * grow_ref[...]
        o_ref[0, :, cols] = (y * sg_ref[0, :, cols].astype(f32)).astype(bf16)


def _attn_sample(q, kn, vn, kc, vc, sg, bias, lamp, grow, lam_init, cache_base):
    b, ss, w = q.shape
    past = kc.shape[1] // DIFF_H
    new_spec = pl.BlockSpec((1, ss, w), lambda bi: (bi, 0, 0))
    cache_spec = pl.BlockSpec((1, past * DIFF_H, HEAD_W), lambda bi: (cache_base + bi, 0, 0))
    return pl.pallas_call(
        functools.partial(_attn_sample_kernel, past=past, lam_init=lam_init),
        grid=(b,),
        in_specs=[new_spec, new_spec, new_spec, cache_spec, cache_spec, new_spec,
                  _full_spec(bias.shape), _full_spec(lamp.shape), _full_spec(grow.shape)],
        out_specs=new_spec,
        out_shape=jax.ShapeDtypeStruct((b, ss, w), bf16),
        compiler_params=_params(1),
        name="attn_sample",
    )(q, kn, vn, kc, vc, sg, bias, lamp, grow)


def _out_kernel(zd_ref, zc_ref, zm_ref, x_ref, w_ref, g_ref, b_ref, y_ref, *, alpha):
    wd = zd_ref.shape[1]
    wc = zc_ref.shape[1]
    o = (_dot(zd_ref[...], w_ref[:wd]) + _dot(zc_ref[...], w_ref[wd:wd + wc])
         + _dot(zm_ref[...], w_ref[wd + wc:]))
    r = alpha * x_ref[...] + o
    mu = jnp.mean(r, axis=-1, keepdims=True)
    c = r - mu
    var = jnp.mean(c * c, axis=-1, keepdims=True)
    y_ref[...] = c * lax.rsqrt(var + EPS) * g_ref[...] + b_ref[...]


def _out_proj(zd, zc, zm, x, w, g, b, tm, alpha):
    rows, d = x.shape
    return pl.pallas_call(
        functools.partial(_out_kernel, alpha=alpha),
        grid=(rows // tm,),
        in_specs=[_row_spec(tm, zd.shape[1]), _row_spec(tm, zc.shape[1]), _row_spec(tm, zm.shape[1]),
                  _row_spec(tm, d), _full_spec(w.shape), _full_spec(g.shape), _full_spec(b.shape)],
        out_specs=_row_spec(tm, d),
        out_shape=jax.ShapeDtypeStruct((rows, d), f32),
        compiler_params=_params(1),
        name="out_proj",
    )(zd, zc, zm, x, w, g, b)


def kernel(x_prompt, x_sample, cache_diff_k, cache_diff_v, cache_conv, cache_mem_k, cache_mem_v, mem_prompt,
           rel_bias_table, w_in, w_mem_kv, conv_w, lambda_q1, lambda_k1, lambda_q2, lambda_k2, subln_g, w_out,
           ln_g, ln_b):
    depth = w_in.shape[0]
    bp, sp, d_model = x_prompt.shape
    bs, ss, _ = x_sample.shape
    past = cache_diff_k.shape[2]
    n_mem = mem_prompt.shape[1]
    diff_w = DIFF_H * HEAD_W
    conv_ch = conv_w.shape[2]
    mem_w = cache_mem_k.shape[3] * cache_mem_k.shape[4]
    mem_dh = cache_mem_k.shape[4]
    alpha = (2 * depth) ** 0.25
    blk = ATTN_BLOCK
    assert sp % blk == 0 and blk % CHUNK == 0 and sp % ROW_TILE == 0 and (bs * ss) % ROW_TILE == 0
    assert conv_w.shape[1] == CONV_K and ss >= CONV_K - 1

    widths = (diff_w, diff_w, diff_w, conv_ch, conv_ch, conv_ch, mem_w, diff_w, conv_ch, mem_w)
    offs = np.concatenate([[0], np.cumsum(widths)])
    c_q, c_k, c_v, c_h, c_bg, c_cg, c_mq, c_gd, c_gc, c_gm = [slice(int(a), int(b)) for a, b in zip(offs[:-1], offs[1:])]

    far_bucket = _t5_bucket(jnp.array([-(blk + 1)], jnp.int32)).astype(jnp.int32)
    bias_p = _bias_tiles(rel_bias_table, _prompt_buckets(blk), far_bucket, True)
    bias_s = _bias_tiles(rel_bias_table, _sample_buckets(past, ss), far_bucket, False)

    xp = x_prompt.reshape(bp * sp, d_model)
    xs = x_sample.reshape(bs * ss, d_model)
    outs = {k: [] for k in ("kp", "vp", "cp", "mkp", "mvp", "ks", "vs", "cs")}

    for l in range(depth):
        lam_init = 0.8 - 0.6 * math.exp(-0.3 * l)
        wl = w_in[l]
        w_qk = jnp.concatenate([wl[:, c_q], wl[:, c_k]], axis=1).astype(bf16)
        w_v = jnp.concatenate([wl[:, c_v], wl[:, c_gd]], axis=1).astype(bf16)
        w_cv = jnp.concatenate([wl[:, c_h], wl[:, c_bg], wl[:, c_cg], wl[:, c_gc]], axis=1).astype(bf16)
        w_mq = jnp.concatenate([wl[:, c_mq], wl[:, c_gm]], axis=1).astype(bf16)
        w_o = w_out[l].astype(bf16)
        w_kv = w_mem_kv[l].astype(bf16)
        lamp = jnp.stack([lambda_q1[l], lambda_k1[l], lambda_q2[l], lambda_k2[l]]).astype(f32)
        gscaled = subln_g[l].astype(f32) * (1.0 - lam_init)
        gcol = jnp.broadcast_to(gscaled[:, None], (HEAD_W, blk))
        grow = gscaled[None, :]
        g_ln = ln_g[l][None, :].astype(f32)
        b_ln = ln_b[l][None, :].astype(f32)
        cw = conv_w[l].astype(f32)

        q, kb, kf = _proj_qk(xp, w_qk, ROW_TILE)
        vt, vf, sgd = _proj_v(xp, w_v, ROW_TILE, bp, True)
        zc, utail = _proj_conv(xp, w_cv, cw, jnp.zeros((8, conv_ch), f32), ROW_TILE, sp)
        mq, sgm = _proj_mem(xp, w_mq, ROW_TILE, mem_dh)
        mk, mv = _memkv(mem_prompt.reshape(bp * n_mem, d_model), w_kv)
        zm = _mem_attn(mq, sgm, mk.reshape(bp, n_mem, mem_w), mv.reshape(bp, n_mem, mem_w), ROW_TILE, mem_dh, bp)
        zd = _attn_prompt(q.reshape(bp, sp, diff_w), kb.reshape(bp, sp, diff_w), vt,
                          sgd.reshape(bp, sp, diff_w), bias_p, lamp, gcol, lam_init)
        outs["kp"].append(kf.reshape(bp, sp, DIFF_H, HEAD_W))
        outs["vp"].append(vf.reshape(bp, sp, DIFF_H, HEAD_W))
        outs["cp"].append(utail[:, 8 - (CONV_K - 1):, :])
        outs["mkp"].append(mk.reshape(bp, n_mem, MEM_H, mem_dh))
        outs["mvp"].append(mv.reshape(bp, n_mem, MEM_H, mem_dh))
        xp = _out_proj(zd.reshape(bp * sp, diff_w), zc, zm, xp, w_o, g_ln, b_ln, ROW_TILE, alpha)

        q, kb, kf = _proj_qk(xs, w_qk, ROW_TILE)
        vb, vf, sgd = _proj_v(xs, w_v, ROW_TILE, bs, False)
        conv_init = jnp.pad(cache_conv[l].astype(f32), ((0, 0), (0, ss - (CONV_K - 1)), (0, 0)))
        zc, u = _proj_conv(xs, w_cv, cw, conv_init.reshape(bs * ss, conv_ch), ROW_TILE, ss)
        mq, sgm = _proj_mem(xs, w_mq, ROW_TILE, mem_dh)
        zm = _mem_attn(mq, sgm, cache_mem_k.reshape(depth * bs, n_mem, mem_w),
                       cache_mem_v.reshape(depth * bs, n_mem, mem_w), ss, mem_dh, bs, l * bs)
        zd = _attn_sample(q.reshape(bs, ss, diff_w), kb.reshape(bs, ss, diff_w), vb.reshape(bs, ss, diff_w),
                          cache_diff_k.reshape(depth * bs, past * DIFF_H, HEAD_W),
                          cache_diff_v.reshape(depth * bs, past * DIFF_H, HEAD_W),
                          sgd.reshape(bs, ss, diff_w), bias_s, lamp, grow, lam_init, l * bs)
        outs["ks"].append(kf.reshape(bs, ss, DIFF_H, HEAD_W))
        outs["vs"].append(vf.reshape(bs, ss, DIFF_H, HEAD_W))
        outs["cs"].append(u.reshape(bs, ss, conv_ch)[:, ss - (CONV_K - 1):, :])
        xs = _out_proj(zd.reshape(bs * ss, diff_w), zc, zm, xs, w_o, g_ln, b_ln, ROW_TILE, alpha)

    return (xp.reshape(bp, sp, d_model), xs.reshape(bs, ss, d_model),
            jnp.stack(outs["kp"]), jnp.stack(outs["vp"]), jnp.stack(outs["cp"]), jnp.stack(outs["mkp"]),
            jnp.stack(outs["mvp"]), jnp.stack(outs["ks"]), jnp.stack(outs["vs"]), jnp.stack(outs["cs"]))
```

```python
import functools
import math

import jax
import jax.numpy as jnp
import numpy as np
from jax import lax
from jax.experimental import pallas as pl
from jax.experimental.pallas import tpu as pltpu

f32 = jnp.float32
bf16 = jnp.bfloat16

CHUNK = 64
DIFF_H = 8
DIFF_DH = 64
HEAD_W = 2 * DIFF_DH
MEM_H = 4
CONV_K = 3
N_BUCKETS = 32
MAX_DISTANCE = 128
EPS = 1e-5

LOG2E = 1.4426950408889634
NEG_BIG = -1e30
MASKED_BUCKET = N_BUCKETS

ROW_TILE = 512
ATTN_BLOCK = 512
FAR_UNROLL = 2
SUM_ROWS = 16
VMEM_LIMIT = 56 * 1024 * 1024


def _params(n_axes):
    return pltpu.CompilerParams(dimension_semantics=("arbitrary",) * n_axes,
                                vmem_limit_bytes=VMEM_LIMIT)


def _silu(x):
    return x / (1.0 + jnp.exp(-x))


def _t5_bucket(rel):
    half = N_BUCKETS // 2
    max_exact = half // 2
    ret = jnp.where(rel > 0, half, 0)
    n = jnp.abs(rel)
    nf = jnp.maximum(n, 1).astype(f32)
    large = max_exact + (jnp.log(nf / max_exact) / math.log(MAX_DISTANCE / max_exact)
                         * (half - max_exact)).astype(jnp.int32)
    large = jnp.minimum(large, half - 1)
    return ret + jnp.where(n < max_exact, n, large)


def _bias_kernel(tab_ref, far_ref, bkt_ref, out_ref, *, subtract_far):
    h = pl.program_id(0)
    bkt = bkt_ref[...]
    val = jnp.zeros(bkt.shape, f32)
    for b in range(N_BUCKETS):
        val = jnp.where(bkt == b, tab_ref[b, h], val)
    if subtract_far:
        val = val - tab_ref[far_ref[0], h]
    out_ref[0] = jnp.where(bkt == MASKED_BUCKET, NEG_BIG, val * LOG2E)


def _bias_tiles(table, buckets, far_bucket, subtract_far):
    r, c = buckets.shape
    return pl.pallas_call(
        functools.partial(_bias_kernel, subtract_far=subtract_far),
        grid=(DIFF_H,),
        in_specs=[pl.BlockSpec(memory_space=pltpu.SMEM),
                  pl.BlockSpec(memory_space=pltpu.SMEM),
                  pl.BlockSpec((r, c), lambda h: (0, 0))],
        out_specs=pl.BlockSpec((1, r, c), lambda h: (h, 0, 0)),
        out_shape=jax.ShapeDtypeStruct((DIFF_H, r, c), f32),
        compiler_params=_params(1),
        name="bias_tiles",
    )(table, far_bucket, buckets)


def _prompt_buckets(blk):
    kk = jnp.arange(2 * blk)[:, None] - blk
    qq = jnp.arange(blk)[None, :]
    bkt = _t5_bucket(kk - qq)
    visible = (kk // CHUNK) <= (qq // CHUNK)
    return jnp.where(visible, bkt, MASKED_BUCKET).astype(jnp.int32)


def _sample_buckets(past, ss):
    qpos = past + jnp.arange(ss)[:, None]
    kpos = jnp.arange(past + ss)[None, :]
    bkt = _t5_bucket(kpos - qpos)
    visible = (kpos // CHUNK) <= (qpos // CHUNK)
    bkt = jnp.where(visible, bkt, MASKED_BUCKET).astype(jnp.int32)
    return jnp.concatenate([bkt, bkt], axis=0)


def _dot(a, b):
    return jnp.dot(a, b, preferred_element_type=f32)


def _proj_qk_kernel(x_ref, w_ref, q_ref, kb_ref, kf_ref, *, width, q_scale):
    xb = x_ref[...].astype(bf16)
    q_ref[...] = (_dot(xb, w_ref[:, :width]) * q_scale).astype(bf16)
    k = _dot(xb, w_ref[:, width:])
    kf_ref[...] = k
    kb_ref[...] = k.astype(bf16)


def _proj_v_kernel(x_ref, w_ref, vb_ref, vf_ref, sg_ref, *, width, transpose_v):
    xb = x_ref[...].astype(bf16)
    v = _dot(xb, w_ref[:, :width])
    vf_ref[...] = v
    if transpose_v:
        vb_ref[0] = v.T.astype(bf16)
    else:
        vb_ref[...] = v.astype(bf16)
    sg_ref[...] = _silu(_dot(xb, w_ref[:, width:])).astype(bf16)


def _proj_conv_kernel(x_ref, w_ref, cw_ref, init_ref, z_ref, u_ref, carry_ref, *, width, seq_len):
    rows_n = x_ref.shape[0]
    xb = x_ref[...].astype(bf16)
    h = _dot(xb, w_ref[:, 0 * width:1 * width])
    bg = _dot(xb, w_ref[:, 1 * width:2 * width])
    cg = _dot(xb, w_ref[:, 2 * width:3 * width])
    gc = _dot(xb, w_ref[:, 3 * width:4 * width])
    u = cg * h
    rows = lax.broadcasted_iota(jnp.int32, u.shape, 0)
    if seq_len >= rows_n:
        tiles_per_seq = seq_len // rows_n
        @pl.when(pl.program_id(0) % tiles_per_seq == 0)
        def _():
            carry_ref[...] = init_ref[...]

        carry = carry_ref[...]
        prev1 = jnp.where(rows == 0, carry[7:8], pltpu.roll(u, 1, 0))
        prev2 = jnp.where(rows == 0, carry[6:7], jnp.where(rows == 1, carry[7:8], pltpu.roll(u, 2, 0)))
        carry_ref[...] = u[rows_n - 8:]
        u_ref[0] = u[rows_n - 8:]
    else:
        t = rows % seq_len
        init = init_ref[...]
        prev1 = jnp.where(t == 0, pltpu.roll(init, rows_n - 1, 0), pltpu.roll(u, 1, 0))
        prev2 = jnp.where(t < 2, init, pltpu.roll(u, 2, 0))
        u_ref[...] = u
    cw = cw_ref[...]
    y = cw[0:1] * prev2 + cw[1:2] * prev1 + cw[2:3] * u
    z_ref[...] = ((bg * y) * _silu(gc)).astype(bf16)


def _proj_mem_kernel(x_ref, w_ref, mq_ref, sg_ref, *, width, q_scale):
    xb = x_ref[...].astype(bf16)
    mq_ref[...] = (_dot(xb, w_ref[:, :width]) * q_scale).astype(bf16)
    sg_ref[...] = _silu(_dot(xb, w_ref[:, width:])).astype(bf16)


def _row_spec(tm, n):
    return pl.BlockSpec((tm, n), lambda m: (m, 0))


def _full_spec(shape):
    return pl.BlockSpec(shape, lambda m: (0,) * len(shape))


def _proj_qk(x, w, tm):
    rows, d = x.shape
    width = w.shape[1] // 2
    sds = jax.ShapeDtypeStruct
    return pl.pallas_call(
        functools.partial(_proj_qk_kernel, width=width, q_scale=DIFF_DH ** -0.5 * LOG2E),
        grid=(rows // tm,),
        in_specs=[_row_spec(tm, d), _full_spec(w.shape)],
        out_specs=[_row_spec(tm, width)] * 3,
        out_shape=[sds((rows, width), bf16), sds((rows, width), bf16), sds((rows, width), f32)],
        compiler_params=_params(1),
        name="proj_qk",
    )(x, w)


def _proj_v(x, w, tm, n_seq, transpose_v):
    rows, d = x.shape
    width = w.shape[1] // 2
    seq = rows // n_seq
    sds = jax.ShapeDtypeStruct
    if transpose_v:
        tiles = seq // tm
        vb_spec = pl.BlockSpec((1, width, tm), lambda m: (m // tiles, 0, m % tiles))
        vb_shape = sds((n_seq, width, seq), bf16)
    else:
        vb_spec = _row_spec(tm, width)
        vb_shape = sds((rows, width), bf16)
    return pl.pallas_call(
        functools.partial(_proj_v_kernel, width=width, transpose_v=transpose_v),
        grid=(rows // tm,),
        in_specs=[_row_spec(tm, d), _full_spec(w.shape)],
        out_specs=[vb_spec, _row_spec(tm, width), _row_spec(tm, width)],
        out_shape=[vb_shape, sds((rows, width), f32), sds((rows, width), bf16)],
        compiler_params=_params(1),
        name="proj_v",
    )(x, w)


def _proj_conv(x, w, conv_w, init, tm, seq_len):
    rows, d = x.shape
    width = w.shape[1] // 4
    sds = jax.ShapeDtypeStruct
    if seq_len >= tm:
        tiles = seq_len // tm
        u_spec = pl.BlockSpec((1, 8, width), lambda m: (m // tiles, 0, 0))
        u_shape = sds((rows // seq_len, 8, width), f32)
        init_spec = _full_spec(init.shape)
    else:
        u_spec = _row_spec(tm, width)
        u_shape = sds((rows, width), f32)
        init_spec = _row_spec(tm, width)
    return pl.pallas_call(
        functools.partial(_proj_conv_kernel, width=width, seq_len=seq_len),
        grid=(rows // tm,),
        in_specs=[_row_spec(tm, d), _full_spec(w.shape), _full_spec(conv_w.shape), init_spec],
        out_specs=[_row_spec(tm, width), u_spec],
        out_shape=[sds((rows, width), bf16), u_shape],
        scratch_shapes=[pltpu.VMEM((8, width), f32)],
        compiler_params=_params(1),
        name="proj_conv",
    )(x, w, conv_w, init)


def _proj_mem(x, w, tm, head_dim):
    rows, d = x.shape
    width = w.shape[1] // 2
    sds = jax.ShapeDtypeStruct
    return pl.pallas_call(
        functools.partial(_proj_mem_kernel, width=width, q_scale=head_dim ** -0.5 * LOG2E),
        grid=(rows // tm,),
        in_specs=[_row_spec(tm, d), _full_spec(w.shape)],
        out_specs=[_row_spec(tm, width)] * 2,
        out_shape=[sds((rows, width), bf16)] * 2,
        compiler_params=_params(1),
        name="proj_mem",
    )(x, w)


def _memkv_kernel(x_ref, w_ref, k_ref, v_ref, *, width):
    xb = x_ref[...].astype(bf16)
    k_ref[...] = _dot(xb, w_ref[:, :width])
    v_ref[...] = _dot(xb, w_ref[:, width:])


def _memkv(x, w):
    rows, d = x.shape
    width = w.shape[1] // 2
    sds = jax.ShapeDtypeStruct
    return pl.pallas_call(
        functools.partial(_memkv_kernel, width=width),
        grid=(1,),
        in_specs=[_full_spec(x.shape), _full_spec(w.shape)],
        out_specs=[_full_spec((rows, width))] * 2,
        out_shape=[sds((rows, width), f32)] * 2,
        compiler_params=_params(1),
        name="mem_kv",
    )(x, w)


def _mem_attn_kernel(q_ref, sg_ref, k_ref, v_ref, z_ref, *, head_dim):
    for h in range(q_ref.shape[1] // head_dim):
        cols = slice(h * head_dim, (h + 1) * head_dim)
        kh = k_ref[0, :, cols].astype(bf16)
        vh = v_ref[0, :, cols].astype(bf16)
        s = lax.dot_general(q_ref[:, cols], kh, (((1,), (1,)), ((), ())), preferred_element_type=f32)
        p = jnp.exp2(s - jnp.max(s, axis=-1, keepdims=True))
        o = _dot(p.astype(bf16), vh) / jnp.sum(p, axis=-1, keepdims=True)
        z_ref[:, cols] = (o * sg_ref[:, cols].astype(f32)).astype(bf16)


def _mem_attn(q, sg, mk, mv, tm, head_dim, n_seq, seq_base=0):
    rows, width = q.shape
    n_mem = mk.shape[1]
    tiles = (rows // n_seq) // tm
    kv_spec = pl.BlockSpec((1, n_mem, width), lambda m: (seq_base + m // tiles, 0, 0))
    return pl.pallas_call(
        functools.partial(_mem_attn_kernel, head_dim=head_dim),
        grid=(rows // tm,),
        in_specs=[_row_spec(tm, width), _row_spec(tm, width), kv_spec, kv_spec],
        out_specs=_row_spec(tm, width),
        out_shape=jax.ShapeDtypeStruct((rows, width), bf16),
        compiler_params=_params(1),
        name="mem_attn",
    )(q, sg, mk, mv)


def _lambda(lamp_ref, lam_init):
    lp = lamp_ref[...]
    s1 = jnp.sum(lp[0:1] * lp[1:2], axis=1, keepdims=True)
    s2 = jnp.sum(lp[2:3] * lp[3:4], axis=1, keepdims=True)
    return jnp.exp(s1) - jnp.exp(s2) + lam_init


def _two_maps(q):
    lane = lax.broadcasted_iota(jnp.int32, q.shape, 1)
    zero = jnp.zeros_like(q)
    return jnp.concatenate([jnp.where(lane < DIFF_DH, q, zero), jnp.where(lane >= DIFF_DH, q, zero)], axis=0)


def _attn_prompt_kernel(q_ref, k_ref, vt_ref, sg_ref, bias_ref, lamp_ref, gcol_ref, o_ref,
                        m_ref, mx_ref, alpha_ref, acc_ref, s_ref, p_ref, *, blk, lam_init):
    i = pl.program_id(2)
    qcat = _two_maps(q_ref[0])
    ones = jnp.ones((SUM_ROWS, blk), bf16)
    m_ref[...] = jnp.full(m_ref.shape, NEG_BIG, f32)
    acc_ref[...] = jnp.zeros(acc_ref.shape, f32)
    p_ref[...] = jnp.zeros(p_ref.shape, bf16)
    alpha_ref[...] = jnp.ones(alpha_ref.shape, f32)

    def put_scores(j, bias):
        kb = k_ref[0, pl.ds(pl.multiple_of(j * blk, blk), blk), :]
        st = lax.dot_general(kb, qcat, (((1,), (1,)), ((), ())), preferred_element_type=f32)
        if bias is not None:
            st = st + jnp.concatenate([bias, bias], axis=1)
        s_ref[...] = st
        mx_ref[...] = jnp.max(st.reshape(blk // 8, 8, 2 * blk), axis=0)

    def weighted_values(j):
        vtb = jnp.concatenate([vt_ref[0, :, pl.ds(pl.multiple_of(j * blk, blk), blk)], ones], axis=0)
        return _dot(vtb, p_ref[...])[:acc_ref.shape[0]]

    def step(j, next_bias, prefetch):
        pv_prev = weighted_values(jnp.maximum(j - 1, 0))
        m_old = m_ref[...]
        m_new = jnp.maximum(m_old, jnp.max(mx_ref[...], axis=0, keepdims=True))
        p = jnp.exp2(s_ref[...] - m_new).astype(bf16)
        if prefetch:
            put_scores(j + 1, next_bias)
        acc_ref[...] = alpha_ref[...] * acc_ref[...] + pv_prev
        alpha_ref[...] = jnp.exp2(m_old - m_new)
        m_ref[...] = m_new
        p_ref[...] = p

    bias_prev = lambda: bias_ref[0, :blk, :]
    bias_diag = lambda: bias_ref[0, blk:, :]

    @pl.when(i == 0)
    def _():
        put_scores(0, bias_diag())

    @pl.when(i == 1)
    def _():
        put_scores(0, bias_prev())

    @pl.when(i >= 2)
    def _():
        put_scores(0, None)

    n_far = jnp.maximum(i - 2, 0)
    n_groups = n_far // FAR_UNROLL

    def far_group(g, carry):
        for u in range(FAR_UNROLL):
            step(g * FAR_UNROLL + u, None, True)
        return carry

    def far_step(j, carry):
        step(j, None, True)
        return carry

    lax.fori_loop(0, n_groups, far_group, 0)
    lax.fori_loop(n_groups * FAR_UNROLL, n_far, far_step, 0)

    @pl.when(i >= 2)
    def _():
        step(i - 2, bias_prev(), True)
        step(i - 1, bias_diag(), True)

    @pl.when(i == 1)
    def _():
        step(i - 1, bias_diag(), True)

    step(i, None, False)
    acc_ref[...] = alpha_ref[...] * acc_ref[...] + weighted_values(i)

    lam = _lambda(lamp_ref, lam_init)
    ot = acc_ref[:HEAD_W, :] * (1.0 / acc_ref[HEAD_W:HEAD_W + 1, :])
    d = ot[:, :blk] - lam * ot[:, blk:]
    ms = jnp.mean(d * d, axis=0, keepdims=True)
    y = d * lax.rsqrt(ms + EPS) * gcol_ref[...]
    o_ref[0] = (y.T * sg_ref[0].astype(f32)).astype(bf16)


def _attn_prompt(q, k, vt, sg, bias, lamp, gcol, lam_init):
    b, s, w = q.shape
    blk = ATTN_BLOCK
    qspec = pl.BlockSpec((1, blk, HEAD_W), lambda bi, h, i: (bi, i, h))
    return pl.pallas_call(
        functools.partial(_attn_prompt_kernel, blk=blk, lam_init=lam_init),
        grid=(b, DIFF_H, s // blk),
        in_specs=[qspec,
                  pl.BlockSpec((1, s, HEAD_W), lambda bi, h, i: (bi, 0, h)),
                  pl.BlockSpec((1, HEAD_W, s), lambda bi, h, i: (bi, h, 0)),
                  qspec,
                  pl.BlockSpec((1, 2 * blk, blk), lambda bi, h, i: (h, 0, 0)),
                  pl.BlockSpec(lamp.shape, lambda bi, h, i: (0, 0)),
                  pl.BlockSpec(gcol.shape, lambda bi, h, i: (0, 0))],
        out_specs=qspec,
        out_shape=jax.ShapeDtypeStruct((b, s, w), bf16),
        scratch_shapes=[pltpu.VMEM((1, 2 * blk), f32), pltpu.VMEM((8, 2 * blk), f32),
                        pltpu.VMEM((1, 2 * blk), f32),
                        pltpu.VMEM((HEAD_W + 8, 2 * blk), f32),
                        pltpu.VMEM((blk, 2 * blk), f32), pltpu.VMEM((blk, 2 * blk), bf16)],
        compiler_params=_params(3),
        name="attn_prompt",
    )(q, k, vt, sg, bias, lamp, gcol)


def _attn_sample_kernel(q_ref, kn_ref, vn_ref, kc_ref, vc_ref, sg_ref, bias_ref, lamp_ref, grow_ref, o_ref,
                        *, past, lam_init):
    ss = q_ref.shape[1]
    lam = _lambda(lamp_ref, lam_init)
    nt = (((1,), (1,)), ((), ()))
    for h in range(DIFF_H):
        cols = slice(h * HEAD_W, (h + 1) * HEAD_W)
        qcat = _two_maps(q_ref[0, :, cols])
        head_rows = pl.ds(h, past, stride=DIFF_H)
        sc = lax.dot_general(qcat, kc_ref[0, head_rows, :].astype(bf16), nt, preferred_element_type=f32)
        sn = lax.dot_general(qcat, kn_ref[0, :, cols], nt, preferred_element_type=f32)
        sc = sc + bias_ref[h, :, :past]
        sn = sn + bias_ref[h, :, past:]
        m = jnp.maximum(jnp.max(sc, axis=-1, keepdims=True), jnp.max(sn, axis=-1, keepdims=True))
        pc = jnp.exp2(sc - m)
        pn = jnp.exp2(sn - m)
        l = jnp.sum(pc, axis=-1, keepdims=True) + jnp.sum(pn, axis=-1, keepdims=True)
        o = (_dot(pc.astype(bf16), vc_ref[0, head_rows, :].astype(bf16)) + _dot(pn.astype(bf16), vn_ref[0, :, cols])) / l
        d = o[:ss] - lam * o[ss:]
        ms = jnp.mean(d * d, axis=-1, keepdims=True)
        y = d * lax.rsqrt(ms + EPS) * grow_ref[...]
        o_ref[0, :, cols] = (y * sg_ref[0, :, cols].astype(f32)).astype(bf16)


def _attn_sample(q, kn, vn, kc, vc, sg, bias, lamp, grow, lam_init, cache_base):
    b, ss, w = q.shape
    past = kc.shape[1] // DIFF_H
    new_spec = pl.BlockSpec((1, ss, w), lambda bi: (bi, 0, 0))
    cache_spec = pl.BlockSpec((1, past * DIFF_H, HEAD_W), lambda bi: (cache_base + bi, 0, 0))
    return pl.pallas_call(
        functools.partial(_attn_sample_kernel, past=past, lam_init=lam_init),
        grid=(b,),
        in_specs=[new_spec, new_spec, new_spec, cache_spec, cache_spec, new_spec,
                  _full_spec(bias.shape), _full_spec(lamp.shape), _full_spec(grow.shape)],
        out_specs=new_spec,
        out_shape=jax.ShapeDtypeStruct((b, ss, w), bf16),
        compiler_params=_params(1),
        name="attn_sample",
    )(q, kn, vn, kc, vc, sg, bias, lamp, grow)


def _out_kernel(zd_ref, zc_ref, zm_ref, x_ref, w_ref, g_ref, b_ref, y_ref, *, alpha):
    wd = zd_ref.shape[1]
    wc = zc_ref.shape[1]
    o = (_dot(zd_ref[...], w_ref[:wd]) + _dot(zc_ref[...], w_ref[wd:wd + wc])
         + _dot(zm_ref[...], w_ref[wd + wc:]))
    r = alpha * x_ref[...] + o
    mu = jnp.mean(r, axis=-1, keepdims=True)
    c = r - mu
    var = jnp.mean(c * c, axis=-1, keepdims=True)
    y_ref[...] = c * lax.rsqrt(var + EPS) * g_ref[...] + b_ref[...]


def _out_proj(zd, zc, zm, x, w, g, b, tm, alpha):
    rows, d = x.shape
    return pl.pallas_call(
        functools.partial(_out_kernel, alpha=alpha),
        grid=(rows // tm,),
        in_specs=[_row_spec(tm, zd.shape[1]), _row_spec(tm, zc.shape[1]), _row_spec(tm, zm.shape[1]),
                  _row_spec(tm, d), _full_spec(w.shape), _full_spec(g.shape), _full_spec(b.shape)],
        out_specs=_row_spec(tm, d),
        out_shape=jax.ShapeDtypeStruct((rows, d), f32),
        compiler_params=_params(1),
        name="out_proj",
    )(zd, zc, zm, x, w, g, b)


def kernel(x_prompt, x_sample, cache_diff_k, cache_diff_v, cache_conv, cache_mem_k, cache_mem_v, mem_prompt,
           rel_bias_table, w_in, w_mem_kv, conv_w, lambda_q1, lambda_k1, lambda_q2, lambda_k2, subln_g, w_out,
           ln_g, ln_b):
    depth = w_in.shape[0]
    bp, sp, d_model = x_prompt.shape
    bs, ss, _ = x_sample.shape
    past = cache_diff_k.shape[2]
    n_mem = mem_prompt.shape[1]
    diff_w = DIFF_H * HEAD_W
    conv_ch = conv_w.shape[2]
    mem_w = cache_mem_k.shape[3] * cache_mem_k.shape[4]
    mem_dh = cache_mem_k.shape[4]
    alpha = (2 * depth) ** 0.25
    blk = ATTN_BLOCK
    assert sp % blk == 0 and blk % CHUNK == 0 and sp % ROW_TILE == 0 and (bs * ss) % ROW_TILE == 0
    assert conv_w.shape[1] == CONV_K and ss >= CONV_K - 1

    widths = (diff_w, diff_w, diff_w, conv_ch, conv_ch, conv_ch, mem_w, diff_w, conv_ch, mem_w)
    offs = np.concatenate([[0], np.cumsum(widths)])
    c_q, c_k, c_v, c_h, c_bg, c_cg, c_mq, c_gd, c_gc, c_gm = [slice(int(a), int(b)) for a, b in zip(offs[:-1], offs[1:])]

    far_bucket = _t5_bucket(jnp.array([-(blk + 1)], jnp.int32)).astype(jnp.int32)
    bias_p = _bias_tiles(rel_bias_table, _prompt_buckets(blk), far_bucket, True)
    bias_s = _bias_tiles(rel_bias_table, _sample_buckets(past, ss), far_bucket, False)

    xp = x_prompt.reshape(bp * sp, d_model)
    xs = x_sample.reshape(bs * ss, d_model)
    outs = {k: [] for k in ("kp", "vp", "cp", "mkp", "mvp", "ks", "vs", "cs")}

    for l in range(depth):
        lam_init = 0.8 - 0.6 * math.exp(-0.3 * l)
        wl = w_in[l]
        w_qk = jnp.concatenate([wl[:, c_q], wl[:, c_k]], axis=1).astype(bf16)
        w_v = jnp.concatenate([wl[:, c_v], wl[:, c_gd]], axis=1).astype(bf16)
        w_cv = jnp.concatenate([wl[:, c_h], wl[:, c_bg], wl[:, c_cg], wl[:, c_gc]], axis=1).astype(bf16)
        w_mq = jnp.concatenate([wl[:, c_mq], wl[:, c_gm]], axis=1).astype(bf16)
        w_o = w_out[l].astype(bf16)
        w_kv = w_mem_kv[l].astype(bf16)
        lamp = jnp.stack([lambda_q1[l], lambda_k1[l], lambda_q2[l], lambda_k2[l]]).astype(f32)
        gscaled = subln_g[l].astype(f32) * (1.0 - lam_init)
        gcol = jnp.broadcast_to(gscaled[:, None], (HEAD_W, blk))
        grow = gscaled[None, :]
        g_ln = ln_g[l][None, :].astype(f32)
        b_ln = ln_b[l][None, :].astype(f32)
        cw = conv_w[l].astype(f32)

        q, kb, kf = _proj_qk(xp, w_qk, ROW_TILE)
        vt, vf, sgd = _proj_v(xp, w_v, ROW_TILE, bp, True)
        zc, utail = _proj_conv(xp, w_cv, cw, jnp.zeros((8, conv_ch), f32), ROW_TILE, sp)
        mq, sgm = _proj_mem(xp, w_mq, ROW_TILE, mem_dh)
        mk, mv = _memkv(mem_prompt.reshape(bp * n_mem, d_model), w_kv)
        zm = _mem_attn(mq, sgm, mk.reshape(bp, n_mem, mem_w), mv.reshape(bp, n_mem, mem_w), ROW_TILE, mem_dh, bp)
        zd = _attn_prompt(q.reshape(bp, sp, diff_w), kb.reshape(bp, sp, diff_w), vt,
                          sgd.reshape(bp, sp, diff_w), bias_p, lamp, gcol, lam_init)
        outs["kp"].append(kf.reshape(bp, sp, DIFF_H, HEAD_W))
        outs["vp"].append(vf.reshape(bp, sp, DIFF_H, HEAD_W))
        outs["cp"].append(utail[:, 8 - (CONV_K - 1):, :])
        outs["mkp"].append(mk.reshape(bp, n_mem, MEM_H, mem_dh))
        outs["mvp"].append(mv.reshape(bp, n_mem, MEM_H, mem_dh))
        xp = _out_proj(zd.reshape(bp * sp, diff_w), zc, zm, xp, w_o, g_ln, b_ln, ROW_TILE, alpha)

        q, kb, kf = _proj_qk(xs, w_qk, ROW_TILE)
        vb, vf, sgd = _proj_v(xs, w_v, ROW_TILE, bs, False)
        conv_init = jnp.pad(cache_conv[l].astype(f32), ((0, 0), (0, ss - (CONV_K - 1)), (0, 0)))
        zc, u = _proj_conv(xs, w_cv, cw, conv_init.reshape(bs * ss, conv_ch), ROW_TILE, ss)
        mq, sgm = _proj_mem(xs, w_mq, ROW_TILE, mem_dh)
        zm = _mem_attn(mq, sgm, cache_mem_k.reshape(depth * bs, n_mem, mem_w),
                       cache_mem_v.reshape(depth * bs, n_mem, mem_w), ss, mem_dh, bs, l * bs)
        zd = _attn_sample(q.reshape(bs, ss, diff_w), kb.reshape(bs, ss, diff_w), vb.reshape(bs, ss, diff_w),
                          cache_diff_k.reshape(depth * bs, past * DIFF_H, HEAD_W),
                          cache_diff_v.reshape(depth * bs, past * DIFF_H, HEAD_W),
                          sgd.reshape(bs, ss, diff_w), bias_s, lamp, grow, lam_init, l * bs)
        outs["ks"].append(kf.reshape(bs, ss, DIFF_H, HEAD_W))
        outs["vs"].append(vf.reshape(bs, ss, DIFF_H, HEAD_W))
        outs["cs"].append(u.reshape(bs, ss, conv_ch)[:, ss - (CONV_K - 1):, :])
        xs = _out_proj(zd.reshape(bs * ss, diff_w), zc, zm, xs, w_o, g_ln, b_ln, ROW_TILE, alpha)

    return (xp.reshape(bp, sp, d_model), xs.reshape(bs, ss, d_model),
            jnp.stack(outs["kp"]), jnp.stack(outs["vp"]), jnp.stack(outs["cp"]), jnp.stack(outs["mkp"]),
            jnp.stack(outs["mvp"]), jnp.stack(outs["ks"]), jnp.stack(outs["vs"]), jnp.stack(outs["cs"]))
```
